```python
import math
import jax, jax.numpy as jnp
from jax import lax
import numpy as np

D_MODEL = 1024
BATCH = 4
SEQ = 8192
DEPTH = 1
DEC_BATCH = 128
DEC_SEQ = 1
PAST_LEN = 8192
PAGE_SIZE = 128

D_CONV = D_MODEL // 2
CONV_W = 3
N_HEADS = 4
HEAD_DIM = D_MODEL // 16
V_DIM = 2 * HEAD_DIM
D_ATTN = N_HEADS * V_DIM
D_MIX = D_CONV + D_ATTN
D_PLE = 256
Q_BLOCK = 128
EPS = 1e-6
SPLIT_SIZES = (D_CONV, D_CONV, D_CONV, D_CONV, D_ATTN, D_ATTN, D_ATTN, D_ATTN)
D_IN = sum(SPLIT_SIZES)
SPLIT_POINTS = tuple(int(c) for c in np.cumsum(SPLIT_SIZES)[:-1])

kernel_name = 'hymba_style_conv_diffattn_decoder_step'


def _rmsnorm(x, g):
    xf = x.astype(jnp.float32)
    y = xf * lax.rsqrt(jnp.mean(xf * xf, axis=-1, keepdims=True) + EPS) * g.astype(jnp.float32)
    return y.astype(x.dtype)


def _lambda_init(layer):
    return 0.8 - 0.6 * math.exp(-0.3 * layer)


def _branch_inputs(x, g_norm, w_in, q_norm, k_norm):
    lead = x.shape[:2]
    h = _rmsnorm(x, g_norm) @ w_in
    cb, cc, ch, cg, q, k, v, ag = jnp.split(h, SPLIT_POINTS, axis=-1)
    q = _rmsnorm(q.reshape(*lead, N_HEADS, 2, HEAD_DIM), q_norm)
    k = _rmsnorm(k.reshape(*lead, N_HEADS, 2, HEAD_DIM), k_norm)
    v = v.reshape(*lead, N_HEADS, V_DIM)
    u = cc * ch
    return cb, u, cg, q, k, v, ag


def _causal_conv(u_full, w_conv):
    t = u_full.shape[1] - (CONV_W - 1)
    return sum(w_conv[j] * u_full[:, j:j + t] for j in range(CONV_W))


def _branch_outputs(x, p, cb, conv, cg, o, ag, subln, lam_init, w_out, w_pg, b_pg, w_pp):
    lead = x.shape[:2]
    conv_out = cb * conv * jax.nn.silu(cg)
    o = _rmsnorm(o, subln) * (1.0 - lam_init)
    attn_out = o.reshape(*lead, D_ATTN) * jax.nn.silu(ag)
    h = x + jnp.concatenate([conv_out, attn_out], axis=-1) @ w_out
    gate = jax.nn.sigmoid(h @ w_pg + b_pg)
    return h + gate * (p @ w_pp)


def _diff_attn_prompt(q, k, v, lam):
    b, s = q.shape[:2]
    nb = s // Q_BLOCK
    scale = HEAD_DIM ** -0.5
    qb = q.reshape(b, nb, Q_BLOCK, N_HEADS, 2, HEAD_DIM).swapaxes(0, 1)
    kpos = jnp.arange(s)

    def block(args):
        qi, i = args
        sc = jnp.einsum('bqhcd,bkhcd->bhcqk', qi, k, preferred_element_type=jnp.float32) * scale
        qpos = i * Q_BLOCK + jnp.arange(Q_BLOCK)
        mask = kpos[None, :] <= qpos[:, None]
        sc = jnp.where(mask, sc, jnp.finfo(jnp.float32).min)
        pr = jax.nn.softmax(sc, axis=-1)
        a = pr[:, :, 0] - lam * pr[:, :, 1]
        return jnp.einsum('bhqk,bkhe->bqhe', a.astype(v.dtype), v)

    o = lax.map(block, (qb, jnp.arange(nb)))
    return o.swapaxes(0, 1).reshape(b, s, N_HEADS, V_DIM)


def _diff_attn_sample(q, k_new, v_new, cache_k, cache_v, page_table, lam):
    t = q.shape[1]
    scale = HEAD_DIM ** -0.5
    causal = jnp.tril(jnp.ones((t, t), dtype=bool))

    def one(args):
        qi, kn, vn, pt = args
        kp = cache_k[pt].reshape(-1, N_HEADS, 2, HEAD_DIM)
        vp = cache_v[pt].reshape(-1, N_HEADS, V_DIM)
        s_past = jnp.einsum('qhcd,khcd->hcqk', qi, kp, preferred_element_type=jnp.float32)
        s_new = jnp.einsum('qhcd,khcd->hcqk', qi, kn, preferred_element_type=jnp.float32)
        s_new = jnp.where(causal, s_new, jnp.finfo(jnp.float32).min)
        sc = jnp.concatenate([s_past, s_new], axis=-1) * scale
        pr = jax.nn.softmax(sc, axis=-1)
        a = pr[:, 0] - lam * pr[:, 1]
        vv = jnp.concatenate([vp, vn], axis=0)
        return jnp.einsum('hqk,khe->qhe', a.astype(vv.dtype), vv)

    return lax.map(one, (q, k_new, v_new, page_table))


def setup_inputs(seed: int = 0) -> dict:
    key = jax.random.key(seed)
    ks = jax.random.split(key, 24)
    n_pages = PAST_LEN // PAGE_SIZE
    n_used = DEC_BATCH * n_pages
    n_pool = n_used + max(1, n_used // 4)
    f32 = jnp.float32
    page_table = jax.random.permutation(ks[0], n_pool)[:n_used].reshape(DEC_BATCH, n_pages).astype(jnp.int32)
    return {
        'x_prompt': jax.random.normal(ks[1], (BATCH, SEQ, D_MODEL), f32),
        'x_sample': jax.random.normal(ks[2], (DEC_BATCH, DEC_SEQ, D_MODEL), f32),
        'p_prompt': jax.random.normal(ks[3], (DEPTH, BATCH, SEQ, D_PLE), f32),
        'p_sample': jax.random.normal(ks[4], (DEPTH, DEC_BATCH, DEC_SEQ, D_PLE), f32),
        'cache_k': jax.random.normal(ks[5], (DEPTH, n_pool, PAGE_SIZE, N_HEADS, 2 * HEAD_DIM), f32),
        'cache_v': jax.random.normal(ks[6], (DEPTH, n_pool, PAGE_SIZE, N_HEADS, V_DIM), f32),
        'state_conv': jax.random.normal(ks[7], (DEPTH, DEC_BATCH, CONV_W - 1, D_CONV), f32),
        'page_table': page_table,
        'g_norm': 1.0 + 0.01 * jax.random.normal(ks[8], (DEPTH, D_MODEL), f32),
        'w_in': jax.random.normal(ks[9], (DEPTH, D_MODEL, D_IN), f32) * D_MODEL ** -0.5,
        'w_conv': jax.random.normal(ks[10], (DEPTH, CONV_W, D_CONV), f32) * CONV_W ** -0.5,
        'q_norm': 1.0 + 0.01 * jax.random.normal(ks[11], (DEPTH, HEAD_DIM), f32),
        'k_norm': 1.0 + 0.01 * jax.random.normal(ks[12], (DEPTH, HEAD_DIM), f32),
        'lam_vec': 0.1 * jax.random.normal(ks[13], (DEPTH, 4, HEAD_DIM), f32),
        'subln': 1.0 + 0.01 * jax.random.normal(ks[14], (DEPTH, V_DIM), f32),
        'w_out': jax.random.normal(ks[15], (DEPTH, D_MIX, D_MODEL), f32) * D_MIX ** -0.5,
        'w_pg': jax.random.normal(ks[16], (DEPTH, D_MODEL, D_MODEL), f32) * D_MODEL ** -0.5,
        'b_pg': 0.01 * jax.random.normal(ks[17], (DEPTH, D_MODEL), f32),
        'w_pp': jax.random.normal(ks[18], (DEPTH, D_PLE, D_MODEL), f32) * D_PLE ** -0.5,
    }


def reference(x_prompt, x_sample, p_prompt, p_sample, cache_k, cache_v, state_conv, page_table,
              g_norm, w_in, w_conv, q_norm, k_norm, lam_vec, subln, w_out, w_pg, b_pg, w_pp):
    hp, hs = x_prompt, x_sample
    kp_l, vp_l, cp_l, ks_l, vs_l, cs_l = [], [], [], [], [], []
    for l in range(DEPTH):
        lam_init = _lambda_init(l)
        lv = lam_vec[l].astype(jnp.float32)
        lam = jnp.exp(jnp.sum(lv[0] * lv[1])) - jnp.exp(jnp.sum(lv[2] * lv[3])) + lam_init

        cb, u, cg, q, k, v, ag = _branch_inputs(hp, g_norm[l], w_in[l], q_norm[l], k_norm[l])
        u_full = jnp.concatenate([jnp.zeros((u.shape[0], CONV_W - 1, D_CONV), u.dtype), u], axis=1)
        conv = _causal_conv(u_full, w_conv[l])
        o = _diff_attn_prompt(q, k, v, lam)
        kp_l.append(k.reshape(*k.shape[:3], 2 * HEAD_DIM))
        vp_l.append(v)
        cp_l.append(u_full[:, -(CONV_W - 1):])
        hp = _branch_outputs(hp, p_prompt[l], cb, conv, cg, o, ag, subln[l], lam_init,
                             w_out[l], w_pg[l], b_pg[l], w_pp[l])

        cb, u, cg, q, k, v, ag = _branch_inputs(hs, g_norm[l], w_in[l], q_norm[l], k_norm[l])
        u_full = jnp.concatenate([state_conv[l].astype(u.dtype), u], axis=1)
        conv = _causal_conv(u_full, w_conv[l])
        o = _diff_attn_sample(q, k, v, cache_k[l], cache_v[l], page_table, lam)
        ks_l.append(k.reshape(*k.shape[:3], 2 * HEAD_DIM))
        vs_l.append(v)
        cs_l.append(u_full[:, -(CONV_W - 1):])
        hs = _branch_outputs(hs, p_sample[l], cb, conv, cg, o, ag, subln[l], lam_init,
                             w_out[l], w_pg[l], b_pg[l], w_pp[l])

    k_prompt = jnp.stack(kp_l)
    v_prompt = jnp.stack(vp_l)
    conv_prompt = jnp.stack(cp_l)
    k_sample = jnp.stack(ks_l)
    v_sample = jnp.stack(vs_l)
    conv_sample = jnp.stack(cs_l)
    return (hp, hs, k_prompt, v_prompt, conv_prompt, k_sample, v_sample, conv_sample)
```

```python
import functools
import math

import jax
import jax.numpy as jnp
from jax import lax
from jax.experimental import pallas as pl
from jax.experimental.pallas import tpu as pltpu

F32 = jnp.float32
BF16 = jnp.bfloat16

N_HEADS = 4
HEAD_DIM = 64
V_DIM = 2 * HEAD_DIM
N_MAPS = 2 * N_HEADS
CONV_W = 3
EPS = 1e-6
LANES = 128
SUBLANES = 8
VMEM_LIMIT = 56 * 1024 * 1024

ROW_TILE = 512
ATTN_TILE = 512
DEC_PAGES = 8


def _lambda_init(layer):
    return 0.8 - 0.6 * math.exp(-0.3 * layer)


def _silu(x):
    return x * (1.0 / (1.0 + jnp.exp(-x)))


def _sigmoid(x):
    return 1.0 / (1.0 + jnp.exp(-x))


def _lam(lam_ref, lam_init):
    lv = lam_ref[...]
    a = jnp.sum(lv[0:1] * lv[1:2], axis=1, keepdims=True)
    b = jnp.sum(lv[2:3] * lv[3:4], axis=1, keepdims=True)
    return jnp.exp(a) - jnp.exp(b) + lam_init


def _normed_input(x, g):
    ms = jnp.mean(x * x, axis=-1, keepdims=True)
    return (x * lax.rsqrt(ms + EPS) * g).astype(BF16)


def _proj(xn, w_ref, i, width):
    return jnp.dot(xn, w_ref[:, i * width:(i + 1) * width], preferred_element_type=F32)


def _group_rmsnorm(t, gsum, w):
    ss = jnp.dot((t * t).astype(BF16), gsum, preferred_element_type=F32)
    return t * lax.rsqrt(ss * (1.0 / HEAD_DIM) + EPS) * w


def _prompt_in_kernel(x_ref, g_ref, w_ref, gsum_ref, qw_ref, kw_ref, wc_ref,
                      conv_ref, q_ref, k_ref, kb_ref, v_ref, vb_ref, sag_ref, cst_ref,
                      ubuf):
    j = pl.program_id(1)
    tm = x_ref.shape[1]
    dc = conv_ref.shape[2]

    @pl.when(j == 0)
    def _():
        ubuf[0:SUBLANES, :] = jnp.zeros((SUBLANES, dc), F32)

    xn = _normed_input(x_ref[0], g_ref[...])
    cb = _proj(xn, w_ref, 0, dc)
    u = _proj(xn, w_ref, 1, dc) * _proj(xn, w_ref, 2, dc)
    gate = _silu(_proj(xn, w_ref, 3, dc))
    ubuf[SUBLANES:SUBLANES + tm, :] = u
    wc = wc_ref[...]
    conv = (wc[0:1] * ubuf[SUBLANES - 2:SUBLANES - 2 + tm, :]
            + wc[1:2] * ubuf[SUBLANES - 1:SUBLANES - 1 + tm, :]
            + wc[2:3] * u)
    conv_ref[0] = (cb * conv * gate).astype(BF16)
    ubuf[0:SUBLANES, :] = ubuf[tm:tm + SUBLANES, :]
    cst_ref[0] = u[tm - (CONV_W - 1):tm, :]

    gsum = gsum_ref[...]
    q = _group_rmsnorm(_proj(xn, w_ref, 4, dc), gsum, qw_ref[...])
    q_ref[0] = (q * (HEAD_DIM ** -0.5)).astype(BF16)
    k = _group_rmsnorm(_proj(xn, w_ref, 5, dc), gsum, kw_ref[...])
    k_ref[0] = k
    kb_ref[0] = k.astype(BF16)
    v = _proj(xn, w_ref, 6, dc)
    v_ref[0] = v
    vb_ref[0] = v.astype(BF16)
    sag_ref[0] = _silu(_proj(xn, w_ref, 7, dc)).astype(BF16)


def _prompt_in(x, g, w_in, gsum, qw, kw, wc):
    b, s, d = x.shape
    dc = wc.shape[1]
    tm = ROW_TILE
    const = lambda shape: pl.BlockSpec(shape, lambda i, j: (0,) * len(shape))
    tile = lambda width: pl.BlockSpec((1, tm, width), lambda i, j: (i, j, 0))
    out_shapes = (
        jax.ShapeDtypeStruct((b, s, dc), BF16),
        jax.ShapeDtypeStruct((b, s, dc), BF16),
        jax.ShapeDtypeStruct((b, s, dc), F32),
        jax.ShapeDtypeStruct((b, s, dc), BF16),
        jax.ShapeDtypeStruct((b, s, dc), F32),
        jax.ShapeDtypeStruct((b, s, dc), BF16),
        jax.ShapeDtypeStruct((b, s, dc), BF16),
        jax.ShapeDtypeStruct((b, CONV_W - 1, dc), F32),
    )
    return pl.pallas_call(
        _prompt_in_kernel,
        grid=(b, s // tm),
        in_specs=[tile(d), const((1, d)), const(w_in.shape), const(gsum.shape),
                  const((1, dc)), const((1, dc)), const(wc.shape)],
        out_specs=[tile(dc)] * 7 + [pl.BlockSpec((1, CONV_W - 1, dc), lambda i, j: (i, 0, 0))],
        out_shape=out_shapes,
        scratch_shapes=[pltpu.VMEM((tm + 2 * SUBLANES, dc), F32)],
        compiler_params=pltpu.CompilerParams(
            dimension_semantics=("arbitrary", "arbitrary"), vmem_limit_bytes=VMEM_LIMIT),
        name="prompt_in",
    )(x, g, w_in, gsum, qw, kw, wc)


def _prompt_attn_kernel(q_ref, k_ref, v_ref, sag_ref, lam_ref, subln_ref, o_ref,
                        m_ref, l_ref, acc_ref, *, lam_init):
    qi = pl.program_id(2)
    tq = q_ref.shape[1]
    tk = tq
    rows = 2 * tq

    q = q_ref[0]
    lane = lax.broadcasted_iota(jnp.int32, q.shape, 1)
    zero = jnp.zeros_like(q)
    qs = jnp.concatenate([jnp.where(lane < HEAD_DIM, q, zero),
                          jnp.where(lane >= HEAD_DIM, q, zero)], axis=0)

    m_ref[...] = jnp.full(m_ref.shape, -jnp.inf, F32)
    l_ref[...] = jnp.zeros(l_ref.shape, F32)
    acc_ref[...] = jnp.zeros(acc_ref.shape, F32)

    def step(j, masked):
        start = pl.multiple_of(j * tk, tk)
        kb = k_ref[0, pl.ds(start, tk), :]
        vb = v_ref[0, pl.ds(start, tk), :]
        s = lax.dot_general(qs, kb, (((1,), (1,)), ((), ())), preferred_element_type=F32)
        if masked:
            r = lax.broadcasted_iota(jnp.int32, s.shape, 0) & (tq - 1)
            c = lax.broadcasted_iota(jnp.int32, s.shape, 1)
            s = jnp.where(c <= r, s, -jnp.inf)
        m_prev = m_ref[...]
        m_next = jnp.maximum(m_prev, jnp.max(s, axis=1, keepdims=True))
        alpha = jnp.exp(m_prev - m_next)
        p = jnp.exp(s - pltpu.repeat(m_next, tk // LANES, axis=1))
        l_ref[...] = alpha * l_ref[...] + jnp.sum(p, axis=1, keepdims=True)
        acc_ref[...] = alpha * acc_ref[...] + jnp.dot(p.astype(BF16), vb,
                                                      preferred_element_type=F32)
        m_ref[...] = m_next

    def body(j, carry):
        step(j, False)
        return carry

    lax.fori_loop(0, qi, body, 0)
    step(qi, True)

    o = acc_ref[...] / l_ref[...]
    o = o[:tq] - _lam(lam_ref, lam_init) * o[tq:]
    ms = jnp.mean(o * o, axis=-1, keepdims=True)
    o = o * lax.rsqrt(ms + EPS) * subln_ref[...] * (1.0 - lam_init)
    o_ref[0] = (o * sag_ref[0].astype(F32)).astype(BF16)


def _prompt_attn(q, kb, vb, sag, lam_vec, subln, lam_init):
    b, s, da = q.shape
    t = ATTN_TILE
    assert V_DIM == LANES and da == N_HEADS * V_DIM
    qtile = pl.BlockSpec((1, t, V_DIM), lambda i, h, j: (i, j, h))
    kv = pl.BlockSpec((1, s, V_DIM), lambda i, h, j: (i, 0, h))
    const = lambda shape: pl.BlockSpec(shape, lambda i, h, j: (0,) * len(shape))
    return pl.pallas_call(
        functools.partial(_prompt_attn_kernel, lam_init=lam_init),
        grid=(b, N_HEADS, s // t),
        in_specs=[qtile, kv, kv, qtile, const(lam_vec.shape), const(subln.shape)],
        out_specs=qtile,
        out_shape=jax.ShapeDtypeStruct((b, s, da), BF16),
        scratch_shapes=[pltpu.VMEM((2 * t, LANES), F32)] * 3,
        compiler_params=pltpu.CompilerParams(
            dimension_semantics=("arbitrary",) * 3, vmem_limit_bytes=VMEM_LIMIT),
        name="prompt_attn",
    )(q, kb, vb, sag, lam_vec, subln)


def _out_kernel(x_ref, conv_ref, attn_ref, p_ref, wo_ref, wpg_ref, bpg_ref, wpp_ref, y_ref):
    dc = conv_ref.shape[1]
    h = (x_ref[...]
         + jnp.dot(conv_ref[...], wo_ref[0:dc, :], preferred_element_type=F32)
         + jnp.dot(attn_ref[...], wo_ref[dc:, :], preferred_element_type=F32))
    gate = _sigmoid(jnp.dot(h.astype(BF16), wpg_ref[...], preferred_element_type=F32)
                    + bpg_ref[...])
    emb = jnp.dot(p_ref[...].astype(BF16), wpp_ref[...], preferred_element_type=F32)
    y_ref[...] = h + gate * emb


def _out_proj(x, conv, attn, p, wo, wpg, bpg, wpp):
    n, d = x.shape
    tm = min(ROW_TILE, n)
    const = lambda shape: pl.BlockSpec(shape, lambda i: (0,) * len(shape))
    tile = lambda width: pl.BlockSpec((tm, width), lambda i: (i, 0))
    return pl.pallas_call(
        _out_kernel,
        grid=(n // tm,),
        in_specs=[tile(d), tile(conv.shape[1]), tile(attn.shape[1]), tile(p.shape[1]),
                  const(wo.shape), const(wpg.shape), const(bpg.shape), const(wpp.shape)],
        out_specs=tile(d),
        out_shape=jax.ShapeDtypeStruct((n, d), F32),
        compiler_params=pltpu.CompilerParams(
            dimension_semantics=("arbitrary",), vmem_limit_bytes=VMEM_LIMIT),
        name="out_proj",
    )(x, conv, attn, p, wo, wpg, bpg, wpp)


def _sample_in_kernel(x_ref, g_ref, w_ref, gsum_ref, qw_ref, kw_ref, wc_ref, st0_ref, st1_ref,
                      conv_ref, q_ref, k_ref, v_ref, sag_ref, u_ref):
    dc = conv_ref.shape[1]
    xn = _normed_input(x_ref[...], g_ref[...])
    cb = _proj(xn, w_ref, 0, dc)
    u = _proj(xn, w_ref, 1, dc) * _proj(xn, w_ref, 2, dc)
    gate = _silu(_proj(xn, w_ref, 3, dc))
    wc = wc_ref[...]
    conv = wc[0:1] * st0_ref[...] + wc[1:2] * st1_ref[...] + wc[2:3] * u
    conv_ref[...] = (cb * conv * gate).astype(BF16)
    u_ref[...] = u
    gsum = gsum_ref[...]
    q = _group_rmsnorm(_proj(xn, w_ref, 4, dc), gsum, qw_ref[...])
    q_ref[...] = (q * (HEAD_DIM ** -0.5)).astype(BF16)
    k_ref[...] = _group_rmsnorm(_proj(xn, w_ref, 5, dc), gsum, kw_ref[...])
    v_ref[...] = _proj(xn, w_ref, 6, dc)
    sag_ref[...] = _silu(_proj(xn, w_ref, 7, dc)).astype(BF16)


def _sample_in(x, g, w_in, gsum, qw, kw, wc, st0, st1):
    n, d = x.shape
    dc = wc.shape[1]
    full = lambda a: pl.BlockSpec(a.shape, lambda i: (0,) * a.ndim)
    args = (x, g, w_in, gsum, qw, kw, wc, st0, st1)
    sds = lambda dt: jax.ShapeDtypeStruct((n, dc), dt)
    out_shape = (sds(BF16), sds(BF16), sds(F32), sds(F32), sds(BF16), sds(F32))
    return pl.pallas_call(
        _sample_in_kernel,
        grid=(1,),
        in_specs=[full(a) for a in args],
        out_specs=[pl.BlockSpec((n, dc), lambda i: (0, 0))] * len(out_shape),
        out_shape=out_shape,
        compiler_params=pltpu.CompilerParams(
            dimension_semantics=("arbitrary",), vmem_limit_bytes=VMEM_LIMIT),
        name="sample_in",
    )(*args)


def _sample_attn_kernel(pt_ref, q_ref, kn_ref, vn_ref, sag_ref, lam_ref, subln_ref, *rest,
                        pages, lam_init):
    k_refs = rest[:pages]
    v_refs = rest[pages:2 * pages]
    o_ref = rest[2 * pages]
    m_ref, l_ref, acc_ref = rest[2 * pages + 1:]
    j = pl.program_id(1)
    da = q_ref.shape[2]

    row = lax.broadcasted_iota(jnp.int32, (N_MAPS, da), 0)
    col = lax.broadcasted_iota(jnp.int32, (N_MAPS, da), 1)
    own = (col // HEAD_DIM) == row
    qf = jnp.where(own, jnp.broadcast_to(q_ref[0].astype(F32), (N_MAPS, da)), 0.0)
    qz = qf.astype(BF16)

    @pl.when(j == 0)
    def _():
        s_new = jnp.sum(qf * kn_ref[0], axis=1, keepdims=True)
        m_ref[...] = jnp.broadcast_to(s_new, m_ref.shape)
        l_ref[...] = jnp.ones(l_ref.shape, F32)
        acc_ref[...] = jnp.broadcast_to(vn_ref[0], acc_ref.shape)

    s = jnp.concatenate(
        [lax.dot_general(qz, k_refs[i][0].astype(BF16), (((1,), (1,)), ((), ())),
                         preferred_element_type=F32) for i in range(pages)], axis=1)
    m_prev = m_ref[:, 0:1]
    m_next = jnp.maximum(m_prev, jnp.max(s, axis=1, keepdims=True))
    alpha = jnp.exp(m_prev - m_next)
    p = jnp.exp(s - m_next)
    l_next = alpha * l_ref[:, 0:1] + jnp.sum(p, axis=1, keepdims=True)
    pb = p.astype(BF16)
    acc = alpha * acc_ref[...]
    page = k_refs[0].shape[1]
    for i in range(pages):
        acc = acc + jnp.dot(pb[:, i * page:(i + 1) * page], v_refs[i][0].astype(BF16),
                            preferred_element_type=F32)
    acc_ref[...] = acc
    m_ref[...] = jnp.broadcast_to(m_next, m_ref.shape)
    l_ref[...] = jnp.broadcast_to(l_next, l_ref.shape)

    @pl.when(j == pl.num_programs(1) - 1)
    def _():
        lam = _lam(lam_ref, lam_init)
        o_all = acc / l_next
        for h in range(N_HEADS):
            sl = slice(h * V_DIM, (h + 1) * V_DIM)
            o = o_all[2 * h:2 * h + 1, sl] - lam * o_all[2 * h + 1:2 * h + 2, sl]
            ms = jnp.mean(o * o, axis=-1, keepdims=True)
            o = o * lax.rsqrt(ms + EPS) * subln_ref[...] * (1.0 - lam_init)
            o_ref[0, :, sl] = (o * sag_ref[0, :, sl].astype(F32)).astype(BF16)


def _sample_attn(page_table, q, kn, vn, sag, lam_vec, subln, cache_k, cache_v, lam_init):
    n, da = q.shape
    n_pages = page_table.shape[1]
    pages = DEC_PAGES
    n_pool, page = cache_k.shape[:2]
    ck = cache_k.reshape(n_pool, page, da)
    cv = cache_v.reshape(n_pool, page, da)
    row = pl.BlockSpec((1, 1, da), lambda i, j, pt: (i, 0, 0))
    const = lambda shape: pl.BlockSpec(shape, lambda i, j, pt: (0,) * len(shape))

    def page_spec(k):
        return pl.BlockSpec((1, page, da), lambda i, j, pt: (pt[i, j * pages + k], 0, 0))

    grid_spec = pltpu.PrefetchScalarGridSpec(
        num_scalar_prefetch=1,
        grid=(n, n_pages // pages),
        in_specs=[row, row, row, row, const(lam_vec.shape), const(subln.shape)]
                 + [page_spec(k) for k in range(pages)] * 2,
        out_specs=row,
        scratch_shapes=[pltpu.VMEM((N_MAPS, LANES), F32), pltpu.VMEM((N_MAPS, LANES), F32),
                        pltpu.VMEM((N_MAPS, da), F32)],
    )
    r3 = lambda a: a.reshape(n, 1, da)
    out = pl.pallas_call(
        functools.partial(_sample_attn_kernel, pages=pages, lam_init=lam_init),
        grid_spec=grid_spec,
        out_shape=jax.ShapeDtypeStruct((n, 1, da), BF16),
        compiler_params=pltpu.CompilerParams(
            dimension_semantics=("arbitrary", "arbitrary"), vmem_limit_bytes=VMEM_LIMIT),
        name="sample_attn",
    )(page_table, r3(q), r3(kn), r3(vn), r3(sag), lam_vec, subln,
      *([ck] * pages), *([cv] * pages))
    return out.reshape(n, da)


def kernel(x_prompt, x_sample, p_prompt, p_sample, cache_k, cache_v, state_conv, page_table,
           g_norm, w_in, w_conv, q_norm, k_norm, lam_vec, subln, w_out, w_pg, b_pg, w_pp):
    depth = w_in.shape[0]
    b, s, d = x_prompt.shape
    db, dt, _ = x_sample.shape
    assert dt == 1
    dc = w_conv.shape[2]
    da = N_HEADS * V_DIM
    grp = jnp.arange(da) // HEAD_DIM
    gsum = (grp[:, None] == grp[None, :]).astype(BF16)

    hp = x_prompt
    hs = x_sample.reshape(db, d)
    outs = [[] for _ in range(6)]
    for l in range(depth):
        lam_init = _lambda_init(l)
        g = g_norm[l].reshape(1, d)
        wi = w_in[l].astype(BF16)
        qw = jnp.tile(q_norm[l], da // HEAD_DIM).reshape(1, da)
        kw = jnp.tile(k_norm[l], da // HEAD_DIM).reshape(1, da)
        sub = subln[l].reshape(1, V_DIM)
        wo = w_out[l].astype(BF16)
        wpg = w_pg[l].astype(BF16)
        bpg = b_pg[l].reshape(1, d)
        wpp = w_pp[l].astype(BF16)

        conv, q, k, kb, v, vb, sag, cst = _prompt_in(hp, g, wi, gsum, qw, kw, w_conv[l])
        attn = _prompt_attn(q, kb, vb, sag, lam_vec[l], sub, lam_init)
        hp = _out_proj(hp.reshape(b * s, d), conv.reshape(b * s, dc), attn.reshape(b * s, da),
                       p_prompt[l].reshape(b * s, -1), wo, wpg, bpg, wpp).reshape(b, s, d)
        outs[0].append(k.reshape(b, s, N_HEADS, 2 * HEAD_DIM))
        outs[1].append(v.reshape(b, s, N_HEADS, V_DIM))
        outs[2].append(cst)

        st = state_conv[l]
        conv_s, q_s, k_s, v_s, sag_s, u_s = _sample_in(
            hs, g, wi, gsum, qw, kw, w_conv[l], st[:, 0], st[:, 1])
        attn_s = _sample_attn(page_table, q_s, k_s, v_s, sag_s, lam_vec[l], sub,
                              cache_k[l], cache_v[l], lam_init)
        hs = _out_proj(hs, conv_s, attn_s, p_sample[l].reshape(db, -1), wo, wpg, bpg, wpp)
        outs[3].append(k_s.reshape(db, 1, N_HEADS, 2 * HEAD_DIM))
        outs[4].append(v_s.reshape(db, 1, N_HEADS, V_DIM))
        outs[5].append(jnp.stack([st[:, 1], u_s], axis=1))

    return (hp, hs.reshape(db, 1, d)) + tuple(jnp.stack(o) for o in outs)
```

```python
import functools
import math

import jax
import jax.numpy as jnp
from jax import lax
from jax.experimental import pallas as pl
from jax.experimental.pallas import tpu as pltpu

F32 = jnp.float32
BF16 = jnp.bfloat16

N_HEADS = 4
HEAD_DIM = 64
V_DIM = 2 * HEAD_DIM
N_MAPS = 2 * N_HEADS
CONV_W = 3
EPS = 1e-6
LANES = 128
SUBLANES = 8
MXU_COLS = 256
VMEM_LIMIT = 56 * 1024 * 1024
Q_SCALE = HEAD_DIM ** -0.5 * math.log2(math.e)

TILE = 512
DEC_PAGES = 16


def _lambda_init(layer):
    return 0.8 - 0.6 * math.exp(-0.3 * layer)


def _silu(x):
    return x * (1.0 / (1.0 + jnp.exp(-x)))


def _sigmoid(x):
    return 1.0 / (1.0 + jnp.exp(-x))


def _lam(lam_ref, lam_init):
    lv = lam_ref[...]
    a = jnp.sum(lv[0:1] * lv[1:2], axis=1, keepdims=True)
    b = jnp.sum(lv[2:3] * lv[3:4], axis=1, keepdims=True)
    return jnp.exp(a) - jnp.exp(b) + lam_init


def _normed_input(x, g):
    ms = jnp.mean(x * x, axis=-1, keepdims=True)
    return (x * lax.rsqrt(ms + EPS) * g).astype(BF16)


def _proj(xn, w_ref, i, width):
    return jnp.dot(xn, w_ref[:, i * width:(i + 1) * width], preferred_element_type=F32)


def _group_rmsnorm(t, gsum, w):
    ss = jnp.dot((t * t).astype(BF16), gsum, preferred_element_type=F32)
    return t * lax.rsqrt(ss * (1.0 / HEAD_DIM) + EPS) * w


def _head(x, h):
    return x[:, h * V_DIM:(h + 1) * V_DIM]


def _prompt_in_kernel(x_ref, g_ref, w_ref, gsum_ref, qw_ref, kw_ref, wc_ref,
                      conv_ref, q_ref, k_ref, kb_ref, v_ref, vt_ref, sag_ref, cst_ref,
                      ubuf):
    j = pl.program_id(1)
    tm = x_ref.shape[1]
    dc = conv_ref.shape[2]

    @pl.when(j == 0)
    def _():
        ubuf[0:SUBLANES, :] = jnp.zeros((SUBLANES, dc), F32)

    xn = _normed_input(x_ref[0], g_ref[...])
    cb = _proj(xn, w_ref, 0, dc)
    u = _proj(xn, w_ref, 1, dc) * _proj(xn, w_ref, 2, dc)
    gate = _silu(_proj(xn, w_ref, 3, dc))
    ubuf[SUBLANES:SUBLANES + tm, :] = u
    wc = wc_ref[...]
    conv = (wc[0:1] * ubuf[SUBLANES - 2:SUBLANES - 2 + tm, :]
            + wc[1:2] * ubuf[SUBLANES - 1:SUBLANES - 1 + tm, :]
            + wc[2:3] * u)
    conv_ref[0] = (cb * conv * gate).astype(BF16)
    ubuf[0:SUBLANES, :] = ubuf[tm:tm + SUBLANES, :]
    cst_ref[0] = u[tm - (CONV_W - 1):tm, :]

    gsum = gsum_ref[...]
    q = _group_rmsnorm(_proj(xn, w_ref, 4, dc), gsum, qw_ref[...])
    q_ref[0] = (q * Q_SCALE).astype(BF16)
    k = _group_rmsnorm(_proj(xn, w_ref, 5, dc), gsum, kw_ref[...])
    kb_ref[0] = k.astype(BF16)
    v = _proj(xn, w_ref, 6, dc)
    for h in range(N_HEADS):
        k_ref[0, pl.ds(h, tm, stride=N_HEADS), :] = _head(k, h)
        v_ref[0, pl.ds(h, tm, stride=N_HEADS), :] = _head(v, h)
        vt_ref[0, h, 0] = _head(v, h).T.astype(BF16)
    sag_ref[0] = _silu(_proj(xn, w_ref, 7, dc)).astype(BF16)


def _prompt_in(x, g, w_in, gsum, qw, kw, wc):
    b, s, d = x.shape
    dc = wc.shape[1]
    tm = TILE
    const = lambda shape: pl.BlockSpec(shape, lambda i, j: (0,) * len(shape))
    tile = lambda width: pl.BlockSpec((1, tm, width), lambda i, j: (i, j, 0))
    native = pl.BlockSpec((1, tm * N_HEADS, V_DIM), lambda i, j: (i, j, 0))
    out_shapes = (
        jax.ShapeDtypeStruct((b, s, dc), BF16),
        jax.ShapeDtypeStruct((b, s, dc), BF16),
        jax.ShapeDtypeStruct((b, s * N_HEADS, V_DIM), F32),
        jax.ShapeDtypeStruct((b, s, dc), BF16),
        jax.ShapeDtypeStruct((b, s * N_HEADS, V_DIM), F32),
        jax.ShapeDtypeStruct((b, N_HEADS, s // tm, V_DIM, tm), BF16),
        jax.ShapeDtypeStruct((b, s, dc), BF16),
        jax.ShapeDtypeStruct((b, CONV_W - 1, dc), F32),
    )
    out_specs = [tile(dc), tile(dc), native, tile(dc), native,
                 pl.BlockSpec((1, N_HEADS, 1, V_DIM, tm), lambda i, j: (i, 0, j, 0, 0)),
                 tile(dc),
                 pl.BlockSpec((1, CONV_W - 1, dc), lambda i, j: (i, 0, 0))]
    return pl.pallas_call(
        _prompt_in_kernel,
        grid=(b, s // tm),
        in_specs=[tile(d), const((1, d)), const(w_in.shape), const(gsum.shape),
                  const((1, dc)), const((1, dc)), const(wc.shape)],
        out_specs=out_specs,
        out_shape=out_shapes,
        scratch_shapes=[pltpu.VMEM((tm + 2 * SUBLANES, dc), F32)],
        compiler_params=pltpu.CompilerParams(
            dimension_semantics=("arbitrary", "arbitrary"), vmem_limit_bytes=VMEM_LIMIT),
        name="prompt_in",
    )(x, g, w_in, gsum, qw, kw, wc)


def _prompt_attn_kernel(q_ref, k_ref, vt_ref, sag_ref, lam_ref, subln_ref, o_ref,
                        m_ref, l_ref, acc_ref, s_ref, p_ref, a_ref, *, lam_init):
    qi = pl.program_id(2)
    tq = q_ref.shape[1]
    tk = tq

    q = q_ref[0]
    lane = lax.broadcasted_iota(jnp.int32, q.shape, 1)
    zero = jnp.zeros_like(q)
    qs = jnp.concatenate([jnp.where(lane < HEAD_DIM, q, zero),
                          jnp.where(lane >= HEAD_DIM, q, zero)], axis=0)

    m_ref[...] = jnp.full(m_ref.shape, -jnp.inf, F32)
    l_ref[...] = jnp.zeros(l_ref.shape, F32)
    acc_ref[...] = jnp.zeros(acc_ref.shape, F32)

    chunks = [slice(c * MXU_COLS, (c + 1) * MXU_COLS) for c in range(2 * tq // MXU_COLS)]

    def scores(j, cols):
        start = pl.multiple_of(j * tk, tk)
        return lax.dot_general(k_ref[0, pl.ds(start, tk), :], qs[cols],
                               (((1,), (1,)), ((), ())), preferred_element_type=F32)

    def softmax_update(cols, masked):
        s = s_ref[:, cols]
        if masked:
            key = lax.broadcasted_iota(jnp.int32, s.shape, 0)
            qry = (lax.broadcasted_iota(jnp.int32, s.shape, 1) + cols.start) & (tq - 1)
            s = jnp.where(key <= qry, s, -jnp.inf)
        m_prev = m_ref[:, cols]
        m_next = jnp.maximum(m_prev, jnp.max(s, axis=0, keepdims=True))
        alpha = jnp.exp2(m_prev - m_next)
        p = jnp.exp2(s - m_next)
        l_ref[:, cols] = alpha * l_ref[:, cols] + jnp.sum(p, axis=0, keepdims=True)
        m_ref[:, cols] = m_next
        return alpha, p.astype(BF16)

    def accumulate(cols, alpha, pb, vt):
        acc_ref[:, cols] = alpha * acc_ref[:, cols] + jnp.dot(vt, pb,
                                                            preferred_element_type=F32)

    for cols in chunks:
        s_ref[:, cols] = scores(0, cols)
    p_ref[...] = jnp.zeros(p_ref.shape, BF16)
    a_ref[...] = jnp.ones(a_ref.shape, F32)

    def body(t, carry):
        vt_prev = vt_ref[0, 0, jnp.maximum(t - 1, 0)]
        for cols in chunks:
            accumulate(cols, a_ref[:, cols], p_ref[:, cols], vt_prev)
            alpha, pb = softmax_update(cols, False)
            p_ref[:, cols] = pb
            a_ref[:, cols] = alpha
            s_ref[:, cols] = scores(t + 1, cols)
        return carry

    lax.fori_loop(0, qi, body, 0)
    vt_prev = vt_ref[0, 0, jnp.maximum(qi - 1, 0)]
    vt_last = vt_ref[0, 0, qi]
    for cols in chunks:
        accumulate(cols, a_ref[:, cols], p_ref[:, cols], vt_prev)
        alpha, pb = softmax_update(cols, True)
        accumulate(cols, alpha, pb, vt_last)

    ot = acc_ref[...] * (1.0 / l_ref[...])
    ot = ot[:, :tq] - _lam(lam_ref, lam_init) * ot[:, tq:]
    ms = jnp.mean(ot * ot, axis=0, keepdims=True)
    ot = ot * lax.rsqrt(ms + EPS) * (subln_ref[...] * (1.0 - lam_init))
    o_ref[0] = (ot.T * sag_ref[0].astype(F32)).astype(BF16)


def _prompt_attn(q, kb, vt, sag, lam_vec, subln_col, lam_init):
    b, s, da = q.shape
    t = vt.shape[-1]
    assert V_DIM == LANES and da == N_HEADS * V_DIM
    qtile = pl.BlockSpec((1, t, V_DIM), lambda i, h, j: (i, j, h))
    const = lambda shape: pl.BlockSpec(shape, lambda i, h, j: (0,) * len(shape))
    return pl.pallas_call(
        functools.partial(_prompt_attn_kernel, lam_init=lam_init),
        grid=(b, N_HEADS, s // t),
        in_specs=[qtile,
                  pl.BlockSpec((1, s, V_DIM), lambda i, h, j: (i, 0, h)),
                  pl.BlockSpec((1, 1, s // t, V_DIM, t), lambda i, h, j: (i, h, 0, 0, 0)),
                  qtile, const(lam_vec.shape), const(subln_col.shape)],
        out_specs=qtile,
        out_shape=jax.ShapeDtypeStruct((b, s, da), BF16),
        scratch_shapes=[pltpu.VMEM((1, 2 * t), F32), pltpu.VMEM((1, 2 * t), F32),
                        pltpu.VMEM((V_DIM, 2 * t), F32), pltpu.VMEM((t, 2 * t), F32),
                        pltpu.VMEM((t, 2 * t), BF16), pltpu.VMEM((1, 2 * t), F32)],
        compiler_params=pltpu.CompilerParams(
            dimension_semantics=("arbitrary",) * 3, vmem_limit_bytes=VMEM_LIMIT),
        name="prompt_attn",
    )(q, kb, vt, sag, lam_vec, subln_col)


def _out_kernel(x_ref, conv_ref, attn_ref, p_ref, wo_ref, wpg_ref, bpg_ref, wpp_ref, y_ref):
    dc = conv_ref.shape[1]
    h = (x_ref[...]
         + jnp.dot(conv_ref[...], wo_ref[0:dc, :], preferred_element_type=F32)
         + jnp.dot(attn_ref[...], wo_ref[dc:, :], preferred_element_type=F32))
    gate = _sigmoid(jnp.dot(h.astype(BF16), wpg_ref[...], preferred_element_type=F32)
                    + bpg_ref[...])
    emb = jnp.dot(p_ref[...].astype(BF16), wpp_ref[...], preferred_element_type=F32)
    y_ref[...] = h + gate * emb


def _out_proj(x, conv, attn, p, wo, wpg, bpg, wpp):
    n, d = x.shape
    tm = min(TILE, n)
    const = lambda shape: pl.BlockSpec(shape, lambda i: (0,) * len(shape))
    tile = lambda width: pl.BlockSpec((tm, width), lambda i: (i, 0))
    return pl.pallas_call(
        _out_kernel,
        grid=(n // tm,),
        in_specs=[tile(d), tile(conv.shape[1]), tile(attn.shape[1]), tile(p.shape[1]),
                  const(wo.shape), const(wpg.shape), const(bpg.shape), const(wpp.shape)],
        out_specs=tile(d),
        out_shape=jax.ShapeDtypeStruct((n, d), F32),
        compiler_params=pltpu.CompilerParams(
            dimension_semantics=("arbitrary",), vmem_limit_bytes=VMEM_LIMIT),
        name="out_proj",
    )(x, conv, attn, p, wo, wpg, bpg, wpp)


def _sample_in_kernel(x_ref, g_ref, w_ref, gsum_ref, qw_ref, kw_ref, wc_ref, st0_ref, st1_ref,
                      conv_ref, q_ref, k_ref, v_ref, sag_ref, u_ref):
    dc = conv_ref.shape[1]
    xn = _normed_input(x_ref[...], g_ref[...])
    cb = _proj(xn, w_ref, 0, dc)
    u = _proj(xn, w_ref, 1, dc) * _proj(xn, w_ref, 2, dc)
    gate = _silu(_proj(xn, w_ref, 3, dc))
    wc = wc_ref[...]
    conv = wc[0:1] * st0_ref[...] + wc[1:2] * st1_ref[...] + wc[2:3] * u
    conv_ref[...] = (cb * conv * gate).astype(BF16)
    u_ref[...] = u
    gsum = gsum_ref[...]
    q = _group_rmsnorm(_proj(xn, w_ref, 4, dc), gsum, qw_ref[...])
    q_ref[...] = (q * Q_SCALE).astype(BF16)
    k_ref[...] = _group_rmsnorm(_proj(xn, w_ref, 5, dc), gsum, kw_ref[...])
    v_ref[...] = _proj(xn, w_ref, 6, dc)
    sag_ref[...] = _silu(_proj(xn, w_ref, 7, dc)).astype(BF16)


def _sample_in(x, g, w_in, gsum, qw, kw, wc, st0, st1):
    n, d = x.shape
    dc = wc.shape[1]
    full = lambda a: pl.BlockSpec(a.shape, lambda i: (0,) * a.ndim)
    args = (x, g, w_in, gsum, qw, kw, wc, st0, st1)
    sds = lambda dt: jax.ShapeDtypeStruct((n, dc), dt)
    out_shape = (sds(BF16), sds(BF16), sds(F32), sds(F32), sds(BF16), sds(F32))
    return pl.pallas_call(
        _sample_in_kernel,
        grid=(1,),
        in_specs=[full(a) for a in args],
        out_specs=[pl.BlockSpec((n, dc), lambda i: (0, 0))] * len(out_shape),
        out_shape=out_shape,
        compiler_params=pltpu.CompilerParams(
            dimension_semantics=("arbitrary",), vmem_limit_bytes=VMEM_LIMIT),
        name="sample_in",
    )(*args)


def _per_map_rows(x):
    return jnp.concatenate([_head(x, h) for h in range(N_HEADS) for _ in range(2)], axis=0)


def _sample_attn_kernel(pt_ref, q_ref, kn_ref, vn_ref, sag_ref, lam_ref, subln_ref, *rest,
                        pages, lam_init):
    k_refs = rest[:pages]
    v_refs = rest[pages:2 * pages]
    o_ref = rest[2 * pages]
    m_ref, l_ref, acc_ref = rest[2 * pages + 1:]
    j = pl.program_id(1)
    rows_per_page = k_refs[0].shape[1]

    row = lax.broadcasted_iota(jnp.int32, (N_MAPS, V_DIM), 0)
    lane = lax.broadcasted_iota(jnp.int32, (N_MAPS, V_DIM), 1)
    own = (lane // HEAD_DIM) == (row % 2)
    qf = jnp.where(own, _per_map_rows(q_ref[0].astype(F32)), 0.0)
    qz = qf.astype(BF16)

    @pl.when(j == 0)
    def _():
        s_new = jnp.sum(qf * _per_map_rows(kn_ref[0]), axis=1, keepdims=True)
        m_ref[...] = jnp.broadcast_to(s_new, m_ref.shape)
        l_ref[...] = jnp.ones(l_ref.shape, F32)
        acc_ref[...] = _per_map_rows(vn_ref[0])

    s = jnp.concatenate(
        [lax.dot_general(qz, k_refs[i][0].astype(BF16), (((1,), (1,)), ((), ())),
                         preferred_element_type=F32) for i in range(pages)], axis=1)
    srow = lax.broadcasted_iota(jnp.int32, s.shape, 0)
    scol = lax.broadcasted_iota(jnp.int32, s.shape, 1)
    s = jnp.where((scol % N_HEADS) == (srow // 2), s, -jnp.inf)
    m_prev = m_ref[:, 0:1]
    m_next = jnp.maximum(m_prev, jnp.max(s, axis=1, keepdims=True))
    alpha = jnp.exp2(m_prev - m_next)
    p = jnp.exp2(s - m_next)
    l_next = alpha * l_ref[:, 0:1] + jnp.sum(p, axis=1, keepdims=True)
    pb = p.astype(BF16)
    acc = alpha * acc_ref[...]
    for i in range(pages):
        acc = acc + jnp.dot(pb[:, i * rows_per_page:(i + 1) * rows_per_page],
                            v_refs[i][0].astype(BF16), preferred_element_type=F32)
    acc_ref[...] = acc
    m_ref[...] = jnp.broadcast_to(m_next, m_ref.shape)
    l_ref[...] = jnp.broadcast_to(l_next, l_ref.shape)

    @pl.when(j == pl.num_programs(1) - 1)
    def _():
        lam = _lam(lam_ref, lam_init)
        o_all = acc / l_next
        for h in range(N_HEADS):
            sl = slice(h * V_DIM, (h + 1) * V_DIM)
            o = o_all[2 * h:2 * h + 1] - lam * o_all[2 * h + 1:2 * h + 2]
            ms = jnp.mean(o * o, axis=-1, keepdims=True)
            o = o * lax.rsqrt(ms + EPS) * subln_ref[...] * (1.0 - lam_init)
            o_ref[0, :, sl] = (o * sag_ref[0, :, sl].astype(F32)).astype(BF16)


def _sample_attn(page_table, q, kn, vn, sag, lam_vec, subln, cache_k, cache_v, lam_init):
    n, da = q.shape
    n_pages = page_table.shape[1]
    pages = DEC_PAGES
    n_pool, page = cache_k.shape[:2]
    ck = cache_k.reshape(n_pool, page * N_HEADS, V_DIM)
    cv = cache_v.reshape(n_pool, page * N_HEADS, V_DIM)
    row = pl.BlockSpec((1, 1, da), lambda i, j, pt: (i, 0, 0))
    const = lambda shape: pl.BlockSpec(shape, lambda i, j, pt: (0,) * len(shape))

    def page_spec(k):
        return pl.BlockSpec((1, page * N_HEADS, V_DIM),
                            lambda i, j, pt: (pt[i, j * pages + k], 0, 0))

    grid_spec = pltpu.PrefetchScalarGridSpec(
        num_scalar_prefetch=1,
        grid=(n, n_pages // pages),
        in_specs=[row, row, row, row, const(lam_vec.shape), const(subln.shape)]
                 + [page_spec(k) for k in range(pages)] * 2,
        out_specs=row,
        scratch_shapes=[pltpu.VMEM((N_MAPS, LANES), F32), pltpu.VMEM((N_MAPS, LANES), F32),
                        pltpu.VMEM((N_MAPS, V_DIM), F32)],
    )
    r3 = lambda a: a.reshape(n, 1, da)
    out = pl.pallas_call(
        functools.partial(_sample_attn_kernel, pages=pages, lam_init=lam_init),
        grid_spec=grid_spec,
        out_shape=jax.ShapeDtypeStruct((n, 1, da), BF16),
        compiler_params=pltpu.CompilerParams(
            dimension_semantics=("arbitrary", "arbitrary"), vmem_limit_bytes=VMEM_LIMIT),
        name="sample_attn",
    )(page_table, r3(q), r3(kn), r3(vn), r3(sag), lam_vec, subln,
      *([ck] * pages), *([cv] * pages))
    return out.reshape(n, da)


def kernel(x_prompt, x_sample, p_prompt, p_sample, cache_k, cache_v, state_conv, page_table,
           g_norm, w_in, w_conv, q_norm, k_norm, lam_vec, subln, w_out, w_pg, b_pg, w_pp):
    depth = w_in.shape[0]
    b, s, d = x_prompt.shape
    db, dt, _ = x_sample.shape
    assert dt == 1
    dc = w_conv.shape[2]
    da = N_HEADS * V_DIM
    grp = jnp.arange(da) // HEAD_DIM
    gsum = (grp[:, None] == grp[None, :]).astype(BF16)

    hp = x_prompt
    hs = x_sample.reshape(db, d)
    outs = [[] for _ in range(6)]
    for l in range(depth):
        lam_init = _lambda_init(l)
        g = g_norm[l].reshape(1, d)
        wi = w_in[l].astype(BF16)
        qw = jnp.tile(q_norm[l], da // HEAD_DIM).reshape(1, da)
        kw = jnp.tile(k_norm[l], da // HEAD_DIM).reshape(1, da)
        sub = subln[l].reshape(1, V_DIM)
        wo = w_out[l].astype(BF16)
        wpg = w_pg[l].astype(BF16)
        bpg = b_pg[l].reshape(1, d)
        wpp = w_pp[l].astype(BF16)

        conv, q, k, kb, v, vt, sag, cst = _prompt_in(hp, g, wi, gsum, qw, kw, w_conv[l])
        attn = _prompt_attn(q, kb, vt, sag, lam_vec[l], sub.reshape(V_DIM, 1), lam_init)
        hp = _out_proj(hp.reshape(b * s, d), conv.reshape(b * s, dc), attn.reshape(b * s, da),
                       p_prompt[l].reshape(b * s, -1), wo, wpg, bpg, wpp).reshape(b, s, d)
        outs[0].append(k.reshape(b, s, N_HEADS, 2 * HEAD_DIM))
        outs[1].append(v.reshape(b, s, N_HEADS, V_DIM))
        outs[2].append(cst)

        st = state_conv[l]
        conv_s, q_s, k_s, v_s, sag_s, u_s = _sample_in(
            hs, g, wi, gsum, qw, kw, w_conv[l], st[:, 0], st[:, 1])
        attn_s = _sample_attn(page_table, q_s, k_s, v_s, sag_s, lam_vec[l], sub,
                              cache_k[l], cache_v[l], lam_init)
        hs = _out_proj(hs, conv_s, attn_s, p_sample[l].reshape(db, -1), wo, wpg, bpg, wpp)
        outs[3].append(k_s.reshape(db, 1, N_HEADS, 2 * HEAD_DIM))
        outs[4].append(v_s.reshape(db, 1, N_HEADS, V_DIM))
        outs[5].append(jnp.stack([st[:, 1], u_s], axis=1))

    return (hp, hs.reshape(db, 1, d)) + tuple(jnp.stack(o) for o in outs)
```

```python
import functools
import math

import jax
import jax.numpy as jnp
from jax import lax
from jax.experimental import pallas as pl
from jax.experimental.pallas import tpu as pltpu

F32 = jnp.float32
BF16 = jnp.bfloat16

N_HEADS = 4
HEAD_DIM = 64
V_DIM = 2 * HEAD_DIM
N_MAPS = 2 * N_HEADS
CONV_W = 3
EPS = 1e-6
LANES = 128
SUBLANES = 8
MXU_COLS = 256
VMEM_LIMIT = 56 * 1024 * 1024
Q_SCALE = HEAD_DIM ** -0.5 * math.log2(math.e)

TILE = 512
PAGE_PREFETCH = 2
PAGE_SLOTS = PAGE_PREFETCH + 1


def _lambda_init(layer):
    return 0.8 - 0.6 * math.exp(-0.3 * layer)


def _silu(x):
    return x * (1.0 / (1.0 + jnp.exp(-x)))


def _sigmoid(x):
    return 1.0 / (1.0 + jnp.exp(-x))


def _lam(lam_ref, lam_init):
    lv = lam_ref[...]
    a = jnp.sum(lv[0:1] * lv[1:2], axis=1, keepdims=True)
    b = jnp.sum(lv[2:3] * lv[3:4], axis=1, keepdims=True)
    return jnp.exp(a) - jnp.exp(b) + lam_init


def _normed_input(x, g):
    ms = jnp.mean(x * x, axis=-1, keepdims=True)
    return (x * lax.rsqrt(ms + EPS) * g).astype(BF16)


def _proj(xn, w_ref, i, width):
    return jnp.dot(xn, w_ref[:, i * width:(i + 1) * width], preferred_element_type=F32)


def _group_rmsnorm(t, gsum, w):
    ss = jnp.dot((t * t).astype(BF16), gsum, preferred_element_type=F32)
    return t * lax.rsqrt(ss * (1.0 / HEAD_DIM) + EPS) * w


def _head(x, h):
    return x[:, h * V_DIM:(h + 1) * V_DIM]


def _prompt_in_kernel(x_ref, g_ref, w_ref, gsum_ref, qw_ref, kw_ref, wc_ref,
                      conv_ref, q_ref, k_ref, kb_ref, v_ref, vt_ref, sag_ref, cst_ref,
                      ubuf):
    j = pl.program_id(1)
    tm = x_ref.shape[1]
    dc = conv_ref.shape[2]

    @pl.when(j == 0)
    def _():
        ubuf[0:SUBLANES, :] = jnp.zeros((SUBLANES, dc), F32)

    xn = _normed_input(x_ref[0], g_ref[...])
    cb = _proj(xn, w_ref, 0, dc)
    u = _proj(xn, w_ref, 1, dc) * _proj(xn, w_ref, 2, dc)
    gate = _silu(_proj(xn, w_ref, 3, dc))
    ubuf[SUBLANES:SUBLANES + tm, :] = u
    wc = wc_ref[...]
    conv = (wc[0:1] * ubuf[SUBLANES - 2:SUBLANES - 2 + tm, :]
            + wc[1:2] * ubuf[SUBLANES - 1:SUBLANES - 1 + tm, :]
            + wc[2:3] * u)
    conv_ref[0] = (cb * conv * gate).astype(BF16)
    ubuf[0:SUBLANES, :] = ubuf[tm:tm + SUBLANES, :]
    cst_ref[0] = u[tm - (CONV_W - 1):tm, :]

    gsum = gsum_ref[...]
    q = _group_rmsnorm(_proj(xn, w_ref, 4, dc), gsum, qw_ref[...])
    q_ref[0] = (q * Q_SCALE).astype(BF16)
    k = _group_rmsnorm(_proj(xn, w_ref, 5, dc), gsum, kw_ref[...])
    kb_ref[0] = k.astype(BF16)
    v = _proj(xn, w_ref, 6, dc)
    for h in range(N_HEADS):
        k_ref[0, pl.ds(h, tm, stride=N_HEADS), :] = _head(k, h)
        v_ref[0, pl.ds(h, tm, stride=N_HEADS), :] = _head(v, h)
        vt_ref[0, h, 0] = _head(v, h).T.astype(BF16)
    sag_ref[0] = _silu(_proj(xn, w_ref, 7, dc)).astype(BF16)


def _prompt_in(x, g, w_in, gsum, qw, kw, wc):
    b, s, d = x.shape
    dc = wc.shape[1]
    tm = TILE
    const = lambda shape: pl.BlockSpec(shape, lambda i, j: (0,) * len(shape))
    tile = lambda width: pl.BlockSpec((1, tm, width), lambda i, j: (i, j, 0))
    native = pl.BlockSpec((1, tm * N_HEADS, V_DIM), lambda i, j: (i, j, 0))
    out_shapes = (
        jax.ShapeDtypeStruct((b, s, dc), BF16),
        jax.ShapeDtypeStruct((b, s, dc), BF16),
        jax.ShapeDtypeStruct((b, s * N_HEADS, V_DIM), F32),
        jax.ShapeDtypeStruct((b, s, dc), BF16),
        jax.ShapeDtypeStruct((b, s * N_HEADS, V_DIM), F32),
        jax.ShapeDtypeStruct((b, N_HEADS, s // tm, V_DIM, tm), BF16),
        jax.ShapeDtypeStruct((b, s, dc), BF16),
        jax.ShapeDtypeStruct((b, CONV_W - 1, dc), F32),
    )
    out_specs = [tile(dc), tile(dc), native, tile(dc), native,
                 pl.BlockSpec((1, N_HEADS, 1, V_DIM, tm), lambda i, j: (i, 0, j, 0, 0)),
                 tile(dc),
                 pl.BlockSpec((1, CONV_W - 1, dc), lambda i, j: (i, 0, 0))]
    return pl.pallas_call(
        _prompt_in_kernel,
        grid=(b, s // tm),
        in_specs=[tile(d), const((1, d)), const(w_in.shape), const(gsum.shape),
                  const((1, dc)), const((1, dc)), const(wc.shape)],
        out_specs=out_specs,
        out_shape=out_shapes,
        scratch_shapes=[pltpu.VMEM((tm + 2 * SUBLANES, dc), F32)],
        compiler_params=pltpu.CompilerParams(
            dimension_semantics=("arbitrary", "arbitrary"), vmem_limit_bytes=VMEM_LIMIT),
        name="prompt_in",
    )(x, g, w_in, gsum, qw, kw, wc)


def _sample_in_kernel(x_ref, g_ref, w_ref, gsum_ref, qw_ref, kw_ref, wc_ref, st0_ref, st1_ref,
                      conv_ref, q_ref, k_ref, v_ref, sag_ref, u_ref):
    dc = conv_ref.shape[1]
    xn = _normed_input(x_ref[...], g_ref[...])
    cb = _proj(xn, w_ref, 0, dc)
    u = _proj(xn, w_ref, 1, dc) * _proj(xn, w_ref, 2, dc)
    gate = _silu(_proj(xn, w_ref, 3, dc))
    wc = wc_ref[...]
    conv = wc[0:1] * st0_ref[...] + wc[1:2] * st1_ref[...] + wc[2:3] * u
    conv_ref[...] = (cb * conv * gate).astype(BF16)
    u_ref[...] = u
    gsum = gsum_ref[...]
    q = _group_rmsnorm(_proj(xn, w_ref, 4, dc), gsum, qw_ref[...])
    q_ref[...] = (q * Q_SCALE).astype(BF16)
    k_ref[...] = _group_rmsnorm(_proj(xn, w_ref, 5, dc), gsum, kw_ref[...])
    v_ref[...] = _proj(xn, w_ref, 6, dc)
    sag_ref[...] = _silu(_proj(xn, w_ref, 7, dc)).astype(BF16)


def _sample_in(x, g, w_in, gsum, qw, kw, wc, st0, st1):
    n, d = x.shape
    dc = wc.shape[1]
    full = lambda a: pl.BlockSpec(a.shape, lambda i: (0,) * a.ndim)
    args = (x, g, w_in, gsum, qw, kw, wc, st0, st1)
    sds = lambda dt: jax.ShapeDtypeStruct((n, dc), dt)
    out_shape = (sds(BF16), sds(BF16), sds(F32), sds(F32), sds(BF16), sds(F32))
    return pl.pallas_call(
        _sample_in_kernel,
        grid=(1,),
        in_specs=[full(a) for a in args],
        out_specs=[pl.BlockSpec((n, dc), lambda i: (0, 0))] * len(out_shape),
        out_shape=out_shape,
        compiler_params=pltpu.CompilerParams(
            dimension_semantics=("arbitrary",), vmem_limit_bytes=VMEM_LIMIT),
        name="sample_in",
    )(*args)


def _per_map_rows(x):
    return jnp.concatenate([_head(x, h) for h in range(N_HEADS) for _ in range(2)], axis=0)


def _attn_kernel(pt_ref,
                 qa_ref, qb_ref, k_ref, vt_ref, saga_ref, sagb_ref, lam_ref, subcol_ref,
                 subrow_ref, dq_ref, dkn_ref, dvn_ref, dsag_ref, ck_hbm, cv_hbm,
                 oa_ref, ob_ref, od_ref,
                 m_ref, l_ref, acc_ref, s_ref, p_ref, a_ref,
                 kbuf, vbuf, ksem, vsem, dm_ref, dl_ref, dacc_ref, *, lam_init):
    i = pl.program_id(2)
    nq = vt_ref.shape[2]
    tq = qa_ref.shape[1]
    tk = tq
    seq = (pl.program_id(0) * N_HEADS + pl.program_id(1)) * (nq // 2) + i
    n_seq = pt_ref.shape[0]
    pages_per_chunk = kbuf.shape[1]
    rows_per_page = kbuf.shape[2]
    lam = _lam(lam_ref, lam_init)

    def page_copies(n):
        slot = lax.rem(n, PAGE_SLOTS)
        first_page = lax.rem(n, nq) * pages_per_chunk
        copies = []
        for k in range(pages_per_chunk):
            page = pt_ref[n // nq, first_page + k]
            copies.append(pltpu.make_async_copy(ck_hbm.at[page], kbuf.at[slot, k], ksem.at[slot]))
            copies.append(pltpu.make_async_copy(cv_hbm.at[page], vbuf.at[slot, k], vsem.at[slot]))
        return copies

    @pl.when(seq == 0)
    def _():
        for n in range(PAGE_PREFETCH):
            for cp in page_copies(n):
                cp.start()

    row = lax.broadcasted_iota(jnp.int32, (N_MAPS, V_DIM), 0)
    lane = lax.broadcasted_iota(jnp.int32, (N_MAPS, V_DIM), 1)
    own = (lane // HEAD_DIM) == (row % 2)
    dqf = jnp.where(own, _per_map_rows(dq_ref[0].astype(F32)), 0.0)
    dqz = dqf.astype(BF16)
    s_new = jnp.sum(dqf * _per_map_rows(dkn_ref[0]), axis=1, keepdims=True)
    dm_ref[...] = jnp.broadcast_to(s_new, dm_ref.shape)
    dl_ref[...] = jnp.ones(dl_ref.shape, F32)
    dacc_ref[...] = _per_map_rows(dvn_ref[0])

    def sample_chunk(c):
        n = seq * nq + c

        @pl.when(n + PAGE_PREFETCH < n_seq * nq)
        def _():
            for cp in page_copies(n + PAGE_PREFETCH):
                cp.start()

        for cp in page_copies(n):
            cp.wait()
        slot = lax.rem(n, PAGE_SLOTS)
        rows = pages_per_chunk * rows_per_page
        kc = kbuf[slot].reshape(rows, V_DIM).astype(BF16)
        s = lax.dot_general(dqz, kc, (((1,), (1,)), ((), ())), preferred_element_type=F32)
        yield
        srow = lax.broadcasted_iota(jnp.int32, s.shape, 0)
        scol = lax.broadcasted_iota(jnp.int32, s.shape, 1)
        s = jnp.where((scol % N_HEADS) == (srow // 2), s, -jnp.inf)
        m_prev = dm_ref[:, 0:1]
        m_next = jnp.maximum(m_prev, jnp.max(s, axis=1, keepdims=True))
        alpha = jnp.exp2(m_prev - m_next)
        p = jnp.exp2(s - m_next)
        l_next = alpha * dl_ref[:, 0:1] + jnp.sum(p, axis=1, keepdims=True)
        dm_ref[...] = jnp.broadcast_to(m_next, dm_ref.shape)
        dl_ref[...] = jnp.broadcast_to(l_next, dl_ref.shape)
        yield
        vc = vbuf[slot].reshape(rows, V_DIM).astype(BF16)
        dacc_ref[...] = alpha * dacc_ref[...] + jnp.dot(p.astype(BF16), vc,
                                                        preferred_element_type=F32)
        yield

    def sample_finish():
        o_all = dacc_ref[...] / dl_ref[...]
        for h in range(N_HEADS):
            sl = slice(h * V_DIM, (h + 1) * V_DIM)
            o = o_all[2 * h:2 * h + 1] - lam * o_all[2 * h + 1:2 * h + 2]
            ms = jnp.mean(o * o, axis=-1, keepdims=True)
            o = o * lax.rsqrt(ms + EPS) * subrow_ref[...] * (1.0 - lam_init)
            od_ref[0, :, sl] = (o * dsag_ref[0, :, sl].astype(F32)).astype(BF16)

    chunks = [slice(c * MXU_COLS, (c + 1) * MXU_COLS) for c in range(2 * tq // MXU_COLS)]

    def attend(q_ref, sag_ref, o_ref, qi, first_chunk, last_chunk):
        q = q_ref[0]
        qlane = lax.broadcasted_iota(jnp.int32, q.shape, 1)
        zero = jnp.zeros_like(q)
        qs = jnp.concatenate([jnp.where(qlane < HEAD_DIM, q, zero),
                              jnp.where(qlane >= HEAD_DIM, q, zero)], axis=0)

        m_ref[...] = jnp.full(m_ref.shape, -jnp.inf, F32)
        l_ref[...] = jnp.zeros(l_ref.shape, F32)
        acc_ref[...] = jnp.zeros(acc_ref.shape, F32)

        def scores(j, cols):
            start = pl.multiple_of(j * tk, tk)
            return lax.dot_general(k_ref[0, pl.ds(start, tk), :], qs[cols],
                                   (((1,), (1,)), ((), ())), preferred_element_type=F32)

        def softmax_update(cols, masked):
            s = s_ref[:, cols]
            if masked:
                key = lax.broadcasted_iota(jnp.int32, s.shape, 0)
                qry = (lax.broadcasted_iota(jnp.int32, s.shape, 1) + cols.start) & (tq - 1)
                s = jnp.where(key <= qry, s, -jnp.inf)
            m_prev = m_ref[:, cols]
            m_next = jnp.maximum(m_prev, jnp.max(s, axis=0, keepdims=True))
            alpha = jnp.exp2(m_prev - m_next)
            p = jnp.exp2(s - m_next)
            l_ref[:, cols] = alpha * l_ref[:, cols] + jnp.sum(p, axis=0, keepdims=True)
            m_ref[:, cols] = m_next
            return alpha, p.astype(BF16)

        def accumulate(cols, alpha, pb, vt):
            acc_ref[:, cols] = alpha * acc_ref[:, cols] + jnp.dot(vt, pb,
                                                                preferred_element_type=F32)

        for cols in chunks:
            s_ref[:, cols] = scores(0, cols)
        p_ref[...] = jnp.zeros(p_ref.shape, BF16)
        a_ref[...] = jnp.ones(a_ref.shape, F32)

        def body(t, carry):
            sample = sample_chunk(first_chunk + t)
            vt_prev = vt_ref[0, 0, jnp.maximum(t - 1, 0)]
            for cols in chunks:
                next(sample, None)
                accumulate(cols, a_ref[:, cols], p_ref[:, cols], vt_prev)
                alpha, pb = softmax_update(cols, False)
                p_ref[:, cols] = pb
                a_ref[:, cols] = alpha
                s_ref[:, cols] = scores(t + 1, cols)
            return carry

        lax.fori_loop(0, qi, body, 0)
        sample = iter(()) if last_chunk is None else sample_chunk(last_chunk)
        vt_prev = vt_ref[0, 0, jnp.maximum(qi - 1, 0)]
        vt_last = vt_ref[0, 0, qi]
        for cols in chunks:
            next(sample, None)
            accumulate(cols, a_ref[:, cols], p_ref[:, cols], vt_prev)
            alpha, pb = softmax_update(cols, True)
            accumulate(cols, alpha, pb, vt_last)

        ot = acc_ref[...] * (1.0 / l_ref[...])
        ot = ot[:, :tq] - lam * ot[:, tq:]
        ms = jnp.mean(ot * ot, axis=0, keepdims=True)
        ot = ot * lax.rsqrt(ms + EPS) * (subcol_ref[...] * (1.0 - lam_init))
        o_ref[0] = (ot.T * sag_ref[0].astype(F32)).astype(BF16)

    attend(qa_ref, saga_ref, oa_ref, i, 0, None)
    attend(qb_ref, sagb_ref, ob_ref, nq - 1 - i, i, nq - 1)
    sample_finish()


def _attn(q, kb, vt, sag, lam_vec, subln, page_table, dq, dkn, dvn, dsag, cache_k, cache_v,
          lam_init):
    b, s, da = q.shape
    t = vt.shape[-1]
    nq = s // t
    half = nq // 2
    n_seq, n_pages = page_table.shape
    n_pool, page = cache_k.shape[:2]
    assert V_DIM == LANES and da == N_HEADS * V_DIM and nq % 2 == 0
    assert n_seq == b * N_HEADS * half and n_pages % nq == 0
    pages_per_chunk = n_pages // nq
    rows_per_page = page * N_HEADS
    ck = cache_k.reshape(n_pool, rows_per_page, V_DIM)
    cv = cache_v.reshape(n_pool, rows_per_page, V_DIM)

    qtile = lambda idx: pl.BlockSpec((1, t, V_DIM), idx)
    lo = lambda bi, h, i, pt: (bi, i, h)
    hi = lambda bi, h, i, pt: (bi, nq - 1 - i, h)
    hi_out = lambda bi, h, i, pt: (bi, half - 1 - i, h)
    const = lambda shape: pl.BlockSpec(shape, lambda bi, h, i, pt: (0,) * len(shape))
    seq_row = pl.BlockSpec((1, 1, da), lambda bi, h, i, pt: ((bi * N_HEADS + h) * half + i, 0, 0))
    r3 = lambda a: a.reshape(n_seq, 1, da)
    chunk_buf = pltpu.VMEM((PAGE_SLOTS, pages_per_chunk, rows_per_page, V_DIM), F32)

    grid_spec = pltpu.PrefetchScalarGridSpec(
        num_scalar_prefetch=1,
        grid=(b, N_HEADS, half),
        in_specs=[qtile(lo), qtile(hi),
                  pl.BlockSpec((1, s, V_DIM), lambda bi, h, i, pt: (bi, 0, h)),
                  pl.BlockSpec((1, 1, nq, V_DIM, t), lambda bi, h, i, pt: (bi, h, 0, 0, 0)),
                  qtile(lo), qtile(hi),
                  const(lam_vec.shape), const((V_DIM, 1)), const((1, V_DIM)),
                  seq_row, seq_row, seq_row, seq_row,
                  pl.BlockSpec(memory_space=pl.ANY), pl.BlockSpec(memory_space=pl.ANY)],
        out_specs=[qtile(lo), qtile(hi_out), seq_row],
        scratch_shapes=[pltpu.VMEM((1, 2 * t), F32), pltpu.VMEM((1, 2 * t), F32),
                        pltpu.VMEM((V_DIM, 2 * t), F32), pltpu.VMEM((t, 2 * t), F32),
                        pltpu.VMEM((t, 2 * t), BF16), pltpu.VMEM((1, 2 * t), F32),
                        chunk_buf, chunk_buf,
                        pltpu.SemaphoreType.DMA((PAGE_SLOTS,)),
                        pltpu.SemaphoreType.DMA((PAGE_SLOTS,)),
                        pltpu.VMEM((N_MAPS, LANES), F32), pltpu.VMEM((N_MAPS, LANES), F32),
                        pltpu.VMEM((N_MAPS, V_DIM), F32)],
    )
    o_lo, o_hi, o_s = pl.pallas_call(
        functools.partial(_attn_kernel, lam_init=lam_init),
        grid_spec=grid_spec,
        out_shape=(jax.ShapeDtypeStruct((b, s // 2, da), BF16),
                   jax.ShapeDtypeStruct((b, s // 2, da), BF16),
                   jax.ShapeDtypeStruct((n_seq, 1, da), BF16)),
        compiler_params=pltpu.CompilerParams(
            dimension_semantics=("arbitrary",) * 3, vmem_limit_bytes=VMEM_LIMIT),
        name="attn",
    )(page_table, q, q, kb, vt, sag, sag, lam_vec, subln.reshape(V_DIM, 1), subln,
      r3(dq), r3(dkn), r3(dvn), r3(dsag), ck, cv)
    return jnp.concatenate([o_lo, o_hi], axis=1), o_s.reshape(n_seq, da)


def _out_kernel(x_ref, conv_ref, attn_ref, p_ref, wo_ref, wpg_ref, bpg_ref, wpp_ref, y_ref):
    dc = conv_ref.shape[1]
    h = (x_ref[...]
         + jnp.dot(conv_ref[...], wo_ref[0:dc, :], preferred_element_type=F32)
         + jnp.dot(attn_ref[...], wo_ref[dc:, :], preferred_element_type=F32))
    gate = _sigmoid(jnp.dot(h.astype(BF16), wpg_ref[...], preferred_element_type=F32)
                    + bpg_ref[...])
    emb = jnp.dot(p_ref[...].astype(BF16), wpp_ref[...], preferred_element_type=F32)
    y_ref[...] = h + gate * emb


def _out_proj(x, conv, attn, p, wo, wpg, bpg, wpp):
    n, d = x.shape
    tm = min(TILE, n)
    const = lambda shape: pl.BlockSpec(shape, lambda i: (0,) * len(shape))
    tile = lambda width: pl.BlockSpec((tm, width), lambda i: (i, 0))
    return pl.pallas_call(
        _out_kernel,
        grid=(n // tm,),
        in_specs=[tile(d), tile(conv.shape[1]), tile(attn.shape[1]), tile(p.shape[1]),
                  const(wo.shape), const(wpg.shape), const(bpg.shape), const(wpp.shape)],
        out_specs=tile(d),
        out_shape=jax.ShapeDtypeStruct((n, d), F32),
        compiler_params=pltpu.CompilerParams(
            dimension_semantics=("arbitrary",), vmem_limit_bytes=VMEM_LIMIT),
        name="out_proj",
    )(x, conv, attn, p, wo, wpg, bpg, wpp)


def kernel(x_prompt, x_sample, p_prompt, p_sample, cache_k, cache_v, state_conv, page_table,
           g_norm, w_in, w_conv, q_norm, k_norm, lam_vec, subln, w_out, w_pg, b_pg, w_pp):
    depth = w_in.shape[0]
    b, s, d = x_prompt.shape
    db, dt, _ = x_sample.shape
    assert dt == 1
    dc = w_conv.shape[2]
    da = N_HEADS * V_DIM
    grp = jnp.arange(da) // HEAD_DIM
    gsum = (grp[:, None] == grp[None, :]).astype(BF16)

    hp = x_prompt
    hs = x_sample.reshape(db, d)
    outs = [[] for _ in range(6)]
    for l in range(depth):
        lam_init = _lambda_init(l)
        g = g_norm[l].reshape(1, d)
        wi = w_in[l].astype(BF16)
        qw = jnp.tile(q_norm[l], da // HEAD_DIM).reshape(1, da)
        kw = jnp.tile(k_norm[l], da // HEAD_DIM).reshape(1, da)
        sub = subln[l].reshape(1, V_DIM)
        wo = w_out[l].astype(BF16)
        wpg = w_pg[l].astype(BF16)
        bpg = b_pg[l].reshape(1, d)
        wpp = w_pp[l].astype(BF16)
        st = state_conv[l]

        conv, q, k, kb, v, vt, sag, cst = _prompt_in(hp, g, wi, gsum, qw, kw, w_conv[l])
        conv_s, q_s, k_s, v_s, sag_s, u_s = _sample_in(
            hs, g, wi, gsum, qw, kw, w_conv[l], st[:, 0], st[:, 1])
        attn, attn_s = _attn(q, kb, vt, sag, lam_vec[l], sub, page_table, q_s, k_s, v_s, sag_s,
                             cache_k[l], cache_v[l], lam_init)
        hp = _out_proj(hp.reshape(b * s, d), conv.reshape(b * s, dc), attn.reshape(b * s, da),
                       p_prompt[l].reshape(b * s, -1), wo, wpg, bpg, wpp).reshape(b, s, d)
        hs = _out_proj(hs, conv_s, attn_s, p_sample[l].reshape(db, -1), wo, wpg, bpg, wpp)
        outs[0].append(k.reshape(b, s, N_HEADS, 2 * HEAD_DIM))
        outs[1].append(v.reshape(b, s, N_HEADS, V_DIM))
        outs[2].append(cst)
        outs[3].append(k_s.reshape(db, 1, N_HEADS, 2 * HEAD_DIM))
        outs[4].append(v_s.reshape(db, 1, N_HEADS, V_DIM))
        outs[5].append(jnp.stack([st[:, 1], u_s], axis=1))

    return (hp, hs.reshape(db, 1, d)) + tuple(jnp.stack(o) for o in outs)
```

```python
import functools
import math

import jax
import jax.numpy as jnp
from jax import lax
from jax.experimental import pallas as pl
from jax.experimental.pallas import tpu as pltpu

F32 = jnp.float32
BF16 = jnp.bfloat16

N_HEADS = 4
HEAD_DIM = 64
V_DIM = 2 * HEAD_DIM
N_MAPS = 2 * N_HEADS
CONV_W = 3
EPS = 1e-6
LANES = 128
SUBLANES = 8
MXU_COLS = 256
VMEM_LIMIT = 56 * 1024 * 1024
Q_SCALE = HEAD_DIM ** -0.5 * math.log2(math.e)

TILE = 512
PAGE_PREFETCH = 4
PAGE_SLOTS = PAGE_PREFETCH + 1


def _lambda_init(layer):
    return 0.8 - 0.6 * math.exp(-0.3 * layer)


def _silu(x):
    return x * (1.0 / (1.0 + jnp.exp(-x)))


def _sigmoid(x):
    return 1.0 / (1.0 + jnp.exp(-x))


def _lam(lam_ref, lam_init):
    lv = lam_ref[...]
    a = jnp.sum(lv[0:1] * lv[1:2], axis=1, keepdims=True)
    b = jnp.sum(lv[2:3] * lv[3:4], axis=1, keepdims=True)
    return jnp.exp(a) - jnp.exp(b) + lam_init


def _normed_input(x, g):
    ms = jnp.mean(x * x, axis=-1, keepdims=True)
    return (x * lax.rsqrt(ms + EPS) * g).astype(BF16)


def _proj(xn, w_ref, i, width):
    return jnp.dot(xn, w_ref[:, i * width:(i + 1) * width], preferred_element_type=F32)


def _group_rmsnorm(t, gsum, w):
    ss = jnp.dot((t * t).astype(BF16), gsum, preferred_element_type=F32)
    return t * lax.rsqrt(ss * (1.0 / HEAD_DIM) + EPS) * w


def _head(x, h):
    return x[:, h * V_DIM:(h + 1) * V_DIM]


def _prompt_in_kernel(x_ref, g_ref, w_ref, gsum_ref, qw_ref, kw_ref, wc_ref,
                      conv_ref, q_ref, k_ref, kb_ref, v_ref, vt_ref, sag_ref, cst_ref,
                      ubuf):
    j = pl.program_id(1)
    tm = x_ref.shape[1]
    dc = conv_ref.shape[2]

    @pl.when(j == 0)
    def _():
        ubuf[0:SUBLANES, :] = jnp.zeros((SUBLANES, dc), F32)

    xn = _normed_input(x_ref[0], g_ref[...])
    cb = _proj(xn, w_ref, 0, dc)
    u = _proj(xn, w_ref, 1, dc) * _proj(xn, w_ref, 2, dc)
    gate = _silu(_proj(xn, w_ref, 3, dc))
    ubuf[SUBLANES:SUBLANES + tm, :] = u
    wc = wc_ref[...]
    conv = (wc[0:1] * ubuf[SUBLANES - 2:SUBLANES - 2 + tm, :]
            + wc[1:2] * ubuf[SUBLANES - 1:SUBLANES - 1 + tm, :]
            + wc[2:3] * u)
    conv_ref[0] = (cb * conv * gate).astype(BF16)
    ubuf[0:SUBLANES, :] = ubuf[tm:tm + SUBLANES, :]
    cst_ref[0] = u[tm - (CONV_W - 1):tm, :]

    gsum = gsum_ref[...]
    q = _group_rmsnorm(_proj(xn, w_ref, 4, dc), gsum, qw_ref[...])
    q_ref[0] = (q * Q_SCALE).astype(BF16)
    k = _group_rmsnorm(_proj(xn, w_ref, 5, dc), gsum, kw_ref[...])
    kb_ref[0] = k.astype(BF16)
    v = _proj(xn, w_ref, 6, dc)
    for h in range(N_HEADS):
        k_ref[0, pl.ds(h, tm, stride=N_HEADS), :] = _head(k, h)
        v_ref[0, pl.ds(h, tm, stride=N_HEADS), :] = _head(v, h)
        vt_ref[0, h, 0] = _head(v, h).T.astype(BF16)
    sag_ref[0] = _silu(_proj(xn, w_ref, 7, dc)).astype(BF16)


def _prompt_in(x, g, w_in, gsum, qw, kw, wc):
    b, s, d = x.shape
    dc = wc.shape[1]
    tm = TILE
    const = lambda shape: pl.BlockSpec(shape, lambda i, j: (0,) * len(shape))
    tile = lambda width: pl.BlockSpec((1, tm, width), lambda i, j: (i, j, 0))
    native = pl.BlockSpec((1, tm * N_HEADS, V_DIM), lambda i, j: (i, j, 0))
    out_shapes = (
        jax.ShapeDtypeStruct((b, s, dc), BF16),
        jax.ShapeDtypeStruct((b, s, dc), BF16),
        jax.ShapeDtypeStruct((b, s * N_HEADS, V_DIM), F32),
        jax.ShapeDtypeStruct((b, s, dc), BF16),
        jax.ShapeDtypeStruct((b, s * N_HEADS, V_DIM), F32),
        jax.ShapeDtypeStruct((b, N_HEADS, s // tm, V_DIM, tm), BF16),
        jax.ShapeDtypeStruct((b, s, dc), BF16),
        jax.ShapeDtypeStruct((b, CONV_W - 1, dc), F32),
    )
    out_specs = [tile(dc), tile(dc), native, tile(dc), native,
                 pl.BlockSpec((1, N_HEADS, 1, V_DIM, tm), lambda i, j: (i, 0, j, 0, 0)),
                 tile(dc),
                 pl.BlockSpec((1, CONV_W - 1, dc), lambda i, j: (i, 0, 0))]
    return pl.pallas_call(
        _prompt_in_kernel,
        grid=(b, s // tm),
        in_specs=[tile(d), const((1, d)), const(w_in.shape), const(gsum.shape),
                  const((1, dc)), const((1, dc)), const(wc.shape)],
        out_specs=out_specs,
        out_shape=out_shapes,
        scratch_shapes=[pltpu.VMEM((tm + 2 * SUBLANES, dc), F32)],
        compiler_params=pltpu.CompilerParams(
            dimension_semantics=("arbitrary", "arbitrary"), vmem_limit_bytes=VMEM_LIMIT),
        name="prompt_in",
    )(x, g, w_in, gsum, qw, kw, wc)


def _sample_in_kernel(x_ref, g_ref, w_ref, gsum_ref, qw_ref, kw_ref, wc_ref, st0_ref, st1_ref,
                      conv_ref, q_ref, k_ref, v_ref, sag_ref, u_ref):
    dc = conv_ref.shape[1]
    xn = _normed_input(x_ref[...], g_ref[...])
    cb = _proj(xn, w_ref, 0, dc)
    u = _proj(xn, w_ref, 1, dc) * _proj(xn, w_ref, 2, dc)
    gate = _silu(_proj(xn, w_ref, 3, dc))
    wc = wc_ref[...]
    conv = wc[0:1] * st0_ref[...] + wc[1:2] * st1_ref[...] + wc[2:3] * u
    conv_ref[...] = (cb * conv * gate).astype(BF16)
    u_ref[...] = u
    gsum = gsum_ref[...]
    q = _group_rmsnorm(_proj(xn, w_ref, 4, dc), gsum, qw_ref[...])
    q_ref[...] = (q * Q_SCALE).astype(BF16)
    k_ref[...] = _group_rmsnorm(_proj(xn, w_ref, 5, dc), gsum, kw_ref[...])
    v_ref[...] = _proj(xn, w_ref, 6, dc)
    sag_ref[...] = _silu(_proj(xn, w_ref, 7, dc)).astype(BF16)


def _sample_in(x, g, w_in, gsum, qw, kw, wc, st0, st1):
    n, d = x.shape
    dc = wc.shape[1]
    full = lambda a: pl.BlockSpec(a.shape, lambda i: (0,) * a.ndim)
    args = (x, g, w_in, gsum, qw, kw, wc, st0, st1)
    sds = lambda dt: jax.ShapeDtypeStruct((n, dc), dt)
    out_shape = (sds(BF16), sds(BF16), sds(F32), sds(F32), sds(BF16), sds(F32))
    return pl.pallas_call(
        _sample_in_kernel,
        grid=(1,),
        in_specs=[full(a) for a in args],
        out_specs=[pl.BlockSpec((n, dc), lambda i: (0, 0))] * len(out_shape),
        out_shape=out_shape,
        compiler_params=pltpu.CompilerParams(
            dimension_semantics=("arbitrary",), vmem_limit_bytes=VMEM_LIMIT),
        name="sample_in",
    )(*args)


def _per_map_rows(x):
    return jnp.concatenate([_head(x, h) for h in range(N_HEADS) for _ in range(2)], axis=0)


def _attn_kernel(pt_ref,
                 qa_ref, qb_ref, k_ref, vt_ref, saga_ref, sagb_ref, lam_ref, subcol_ref,
                 subrow_ref, dq_ref, dkn_ref, dvn_ref, dsag_ref, ck_hbm, cv_hbm,
                 oa_ref, ob_ref, od_ref,
                 m_ref, l_ref, acc_ref, s_ref, p_ref, a_ref,
                 kbuf, vbuf, ksem, vsem, dm_ref, dl_ref, dacc_ref, *, lam_init):
    i = pl.program_id(2)
    nq = vt_ref.shape[2]
    tq = qa_ref.shape[1]
    tk = tq
    seq = (pl.program_id(0) * N_HEADS + pl.program_id(1)) * (nq // 2) + i
    n_seq = pt_ref.shape[0]
    pages_per_chunk = kbuf.shape[1]
    rows_per_page = kbuf.shape[2]
    lam = _lam(lam_ref, lam_init)

    def page_copies(n):
        slot = lax.rem(n, PAGE_SLOTS)
        first_page = lax.rem(n, nq) * pages_per_chunk
        copies = []
        for k in range(pages_per_chunk):
            page = pt_ref[n // nq, first_page + k]
            copies.append(pltpu.make_async_copy(ck_hbm.at[page], kbuf.at[slot, k], ksem.at[slot]))
            copies.append(pltpu.make_async_copy(cv_hbm.at[page], vbuf.at[slot, k], vsem.at[slot]))
        return copies

    @pl.when(seq == 0)
    def _():
        for n in range(PAGE_PREFETCH):
            for cp in page_copies(n):
                cp.start()

    row = lax.broadcasted_iota(jnp.int32, (N_MAPS, V_DIM), 0)
    lane = lax.broadcasted_iota(jnp.int32, (N_MAPS, V_DIM), 1)
    own = (lane // HEAD_DIM) == (row % 2)
    dqf = jnp.where(own, _per_map_rows(dq_ref[0].astype(F32)), 0.0)
    dqz = dqf.astype(BF16)
    s_new = jnp.sum(dqf * _per_map_rows(dkn_ref[0]), axis=1, keepdims=True)
    dm_ref[...] = jnp.broadcast_to(s_new, dm_ref.shape)
    dl_ref[...] = jnp.ones(dl_ref.shape, F32)
    dacc_ref[...] = _per_map_rows(dvn_ref[0])

    def sample_chunk(c):
        n = seq * nq + c

        @pl.when(n + PAGE_PREFETCH < n_seq * nq)
        def _():
            for cp in page_copies(n + PAGE_PREFETCH):
                cp.start()

        for cp in page_copies(n):
            cp.wait()
        slot = lax.rem(n, PAGE_SLOTS)
        rows = pages_per_chunk * rows_per_page
        kc = kbuf[slot].reshape(rows, V_DIM).astype(BF16)
        s = lax.dot_general(dqz, kc, (((1,), (1,)), ((), ())), preferred_element_type=F32)
        yield
        srow = lax.broadcasted_iota(jnp.int32, s.shape, 0)
        scol = lax.broadcasted_iota(jnp.int32, s.shape, 1)
        s = jnp.where((scol % N_HEADS) == (srow // 2), s, -jnp.inf)
        m_prev = dm_ref[:, 0:1]
        m_next = jnp.maximum(m_prev, jnp.max(s, axis=1, keepdims=True))
        alpha = jnp.exp2(m_prev - m_next)
        p = jnp.exp2(s - m_next)
        l_next = alpha * dl_ref[:, 0:1] + jnp.sum(p, axis=1, keepdims=True)
        dm_ref[...] = jnp.broadcast_to(m_next, dm_ref.shape)
        dl_ref[...] = jnp.broadcast_to(l_next, dl_ref.shape)
        yield
        vc = vbuf[slot].reshape(rows, V_DIM).astype(BF16)
        dacc_ref[...] = alpha * dacc_ref[...] + jnp.dot(p.astype(BF16), vc,
                                                        preferred_element_type=F32)
        yield

    def sample_finish():
        o_all = dacc_ref[...] / dl_ref[...]
        for h in range(N_HEADS):
            sl = slice(h * V_DIM, (h + 1) * V_DIM)
            o = o_all[2 * h:2 * h + 1] - lam * o_all[2 * h + 1:2 * h + 2]
            ms = jnp.mean(o * o, axis=-1, keepdims=True)
            o = o * lax.rsqrt(ms + EPS) * subrow_ref[...] * (1.0 - lam_init)
            od_ref[0, :, sl] = (o * dsag_ref[0, :, sl].astype(F32)).astype(BF16)

    chunks = [slice(c * MXU_COLS, (c + 1) * MXU_COLS) for c in range(2 * tq // MXU_COLS)]

    def attend(q_ref, sag_ref, o_ref, qi, first_chunk, last_chunk):
        q = q_ref[0]
        qlane = lax.broadcasted_iota(jnp.int32, q.shape, 1)
        zero = jnp.zeros_like(q)
        qs = jnp.concatenate([jnp.where(qlane < HEAD_DIM, q, zero),
                              jnp.where(qlane >= HEAD_DIM, q, zero)], axis=0)

        m_ref[...] = jnp.full(m_ref.shape, -jnp.inf, F32)
        l_ref[...] = jnp.zeros(l_ref.shape, F32)
        acc_ref[...] = jnp.zeros(acc_ref.shape, F32)

        def scores(j, cols):
            start = pl.multiple_of(j * tk, tk)
            return lax.dot_general(k_ref[0, pl.ds(start, tk), :], qs[cols],
                                   (((1,), (1,)), ((), ())), preferred_element_type=F32)

        def softmax_update(cols, masked):
            s = s_ref[:, cols]
            if masked:
                key = lax.broadcasted_iota(jnp.int32, s.shape, 0)
                qry = (lax.broadcasted_iota(jnp.int32, s.shape, 1) + cols.start) & (tq - 1)
                s = jnp.where(key <= qry, s, -jnp.inf)
            m_prev = m_ref[:, cols]
            m_next = jnp.maximum(m_prev, jnp.max(s, axis=0, keepdims=True))
            alpha = jnp.exp2(m_prev - m_next)
            p = jnp.exp2(s - m_next)
            l_ref[:, cols] = alpha * l_ref[:, cols] + jnp.sum(p, axis=0, keepdims=True)
            m_ref[:, cols] = m_next
            return alpha, p.astype(BF16)

        def accumulate(cols, alpha, pb, vt):
            acc_ref[:, cols] = alpha * acc_ref[:, cols] + jnp.dot(vt, pb,
                                                                preferred_element_type=F32)

        for cols in chunks:
            s_ref[:, cols] = scores(0, cols)
        p_ref[...] = jnp.zeros(p_ref.shape, BF16)
        a_ref[...] = jnp.ones(a_ref.shape, F32)

        def body(t, carry):
            sample = sample_chunk(first_chunk + t)
            vt_prev = vt_ref[0, 0, jnp.maximum(t - 1, 0)]
            for cols in chunks:
                next(sample, None)
                accumulate(cols, a_ref[:, cols], p_ref[:, cols], vt_prev)
                alpha, pb = softmax_update(cols, False)
                p_ref[:, cols] = pb
                a_ref[:, cols] = alpha
                s_ref[:, cols] = scores(t + 1, cols)
            return carry

        lax.fori_loop(0, qi, body, 0)
        sample = iter(()) if last_chunk is None else sample_chunk(last_chunk)
        vt_prev = vt_ref[0, 0, jnp.maximum(qi - 1, 0)]
        vt_last = vt_ref[0, 0, qi]
        for cols in chunks:
            next(sample, None)
            accumulate(cols, a_ref[:, cols], p_ref[:, cols], vt_prev)
            alpha, pb = softmax_update(cols, True)
            accumulate(cols, alpha, pb, vt_last)

        ot = acc_ref[...] * (1.0 / l_ref[...])
        ot = ot[:, :tq] - lam * ot[:, tq:]
        ms = jnp.mean(ot * ot, axis=0, keepdims=True)
        ot = ot * lax.rsqrt(ms + EPS) * (subcol_ref[...] * (1.0 - lam_init))
        o_ref[0] = (ot.T * sag_ref[0].astype(F32)).astype(BF16)

    attend(qa_ref, saga_ref, oa_ref, i, 0, None)
    attend(qb_ref, sagb_ref, ob_ref, nq - 1 - i, i, nq - 1)
    sample_finish()


def _attn(q, kb, vt, sag, lam_vec, subln, page_table, dq, dkn, dvn, dsag, cache_k, cache_v,
          lam_init):
    b, s, da = q.shape
    t = vt.shape[-1]
    nq = s // t
    half = nq // 2
    n_seq, n_pages = page_table.shape
    n_pool, page = cache_k.shape[:2]
    assert V_DIM == LANES and da == N_HEADS * V_DIM and nq % 2 == 0
    assert n_seq == b * N_HEADS * half and n_pages % nq == 0
    pages_per_chunk = n_pages // nq
    rows_per_page = page * N_HEADS
    ck = cache_k.reshape(n_pool, rows_per_page, V_DIM)
    cv = cache_v.reshape(n_pool, rows_per_page, V_DIM)

    qtile = lambda idx: pl.BlockSpec((1, t, V_DIM), idx)
    lo = lambda bi, h, i, pt: (bi, i, h)
    hi = lambda bi, h, i, pt: (bi, nq - 1 - i, h)
    hi_out = lambda bi, h, i, pt: (bi, half - 1 - i, h)
    const = lambda shape: pl.BlockSpec(shape, lambda bi, h, i, pt: (0,) * len(shape))
    seq_row = pl.BlockSpec((1, 1, da), lambda bi, h, i, pt: ((bi * N_HEADS + h) * half + i, 0, 0))
    r3 = lambda a: a.reshape(n_seq, 1, da)
    chunk_buf = pltpu.VMEM((PAGE_SLOTS, pages_per_chunk, rows_per_page, V_DIM), F32)

    grid_spec = pltpu.PrefetchScalarGridSpec(
        num_scalar_prefetch=1,
        grid=(b, N_HEADS, half),
        in_specs=[qtile(lo), qtile(hi),
                  pl.BlockSpec((1, s, V_DIM), lambda bi, h, i, pt: (bi, 0, h)),
                  pl.BlockSpec((1, 1, nq, V_DIM, t), lambda bi, h, i, pt: (bi, h, 0, 0, 0)),
                  qtile(lo), qtile(hi),
                  const(lam_vec.shape), const((V_DIM, 1)), const((1, V_DIM)),
                  seq_row, seq_row, seq_row, seq_row,
                  pl.BlockSpec(memory_space=pl.ANY), pl.BlockSpec(memory_space=pl.ANY)],
        out_specs=[qtile(lo), qtile(hi_out), seq_row],
        scratch_shapes=[pltpu.VMEM((1, 2 * t), F32), pltpu.VMEM((1, 2 * t), F32),
                        pltpu.VMEM((V_DIM, 2 * t), F32), pltpu.VMEM((t, 2 * t), F32),
                        pltpu.VMEM((t, 2 * t), BF16), pltpu.VMEM((1, 2 * t), F32),
                        chunk_buf, chunk_buf,
                        pltpu.SemaphoreType.DMA((PAGE_SLOTS,)),
                        pltpu.SemaphoreType.DMA((PAGE_SLOTS,)),
                        pltpu.VMEM((N_MAPS, LANES), F32), pltpu.VMEM((N_MAPS, LANES), F32),
                        pltpu.VMEM((N_MAPS, V_DIM), F32)],
    )
    o_lo, o_hi, o_s = pl.pallas_call(
        functools.partial(_attn_kernel, lam_init=lam_init),
        grid_spec=grid_spec,
        out_shape=(jax.ShapeDtypeStruct((b, s // 2, da), BF16),
                   jax.ShapeDtypeStruct((b, s // 2, da), BF16),
                   jax.ShapeDtypeStruct((n_seq, 1, da), BF16)),
        compiler_params=pltpu.CompilerParams(
            dimension_semantics=("arbitrary",) * 3, vmem_limit_bytes=VMEM_LIMIT),
        name="attn",
    )(page_table, q, q, kb, vt, sag, sag, lam_vec, subln.reshape(V_DIM, 1), subln,
      r3(dq), r3(dkn), r3(dvn), r3(dsag), ck, cv)
    return o_lo, o_hi, o_s.reshape(n_seq, da)


def _out_kernel(x_ref, conv_ref, lo_ref, hi_ref, p_ref, wo_ref, wpg_ref, bpg_ref, wpp_ref, y_ref,
                *, tiles_per_seq):
    dc = conv_ref.shape[1]
    second_half = lax.rem(pl.program_id(0), tiles_per_seq) >= tiles_per_seq // 2
    attn = jnp.where(second_half, hi_ref[...], lo_ref[...])
    h = (x_ref[...]
         + jnp.dot(conv_ref[...], wo_ref[0:dc, :], preferred_element_type=F32)
         + jnp.dot(attn, wo_ref[dc:, :], preferred_element_type=F32))
    gate = _sigmoid(jnp.dot(h.astype(BF16), wpg_ref[...], preferred_element_type=F32)
                    + bpg_ref[...])
    emb = jnp.dot(p_ref[...].astype(BF16), wpp_ref[...], preferred_element_type=F32)
    y_ref[...] = h + gate * emb


def _out_proj(x, conv, attn_lo, attn_hi, p, wo, wpg, bpg, wpp, seq_len):
    n, d = x.shape
    tm = min(TILE, n)
    tps = max(seq_len // tm, 1)
    half = max(tps // 2, 1)
    const = lambda shape: pl.BlockSpec(shape, lambda i: (0,) * len(shape))
    tile = lambda width: pl.BlockSpec((tm, width), lambda i: (i, 0))
    da = attn_lo.shape[1]
    lo = pl.BlockSpec((tm, da), lambda i: ((i // tps) * half + jnp.minimum(i % tps, half - 1), 0))
    hi = pl.BlockSpec((tm, da), lambda i: ((i // tps) * half + jnp.maximum(i % tps - half, 0), 0))
    return pl.pallas_call(
        functools.partial(_out_kernel, tiles_per_seq=tps),
        grid=(n // tm,),
        in_specs=[tile(d), tile(conv.shape[1]), lo, hi, tile(p.shape[1]),
                  const(wo.shape), const(wpg.shape), const(bpg.shape), const(wpp.shape)],
        out_specs=tile(d),
        out_shape=jax.ShapeDtypeStruct((n, d), F32),
        compiler_params=pltpu.CompilerParams(
            dimension_semantics=("arbitrary",), vmem_limit_bytes=VMEM_LIMIT),
        name="out_proj",
    )(x, conv, attn_lo, attn_hi, p, wo, wpg, bpg, wpp)


def kernel(x_prompt, x_sample, p_prompt, p_sample, cache_k, cache_v, state_conv, page_table,
           g_norm, w_in, w_conv, q_norm, k_norm, lam_vec, subln, w_out, w_pg, b_pg, w_pp):
    depth = w_in.shape[0]
    b, s, d = x_prompt.shape
    db, dt, _ = x_sample.shape
    assert dt == 1
    dc = w_conv.shape[2]
    da = N_HEADS * V_DIM
    grp = jnp.arange(da) // HEAD_DIM
    gsum = (grp[:, None] == grp[None, :]).astype(BF16)

    hp = x_prompt
    hs = x_sample.reshape(db, d)
    outs = [[] for _ in range(6)]
    for l in range(depth):
        lam_init = _lambda_init(l)
        g = g_norm[l].reshape(1, d)
        wi = w_in[l].astype(BF16)
        qw = jnp.tile(q_norm[l], da // HEAD_DIM).reshape(1, da)
        kw = jnp.tile(k_norm[l], da // HEAD_DIM).reshape(1, da)
        sub = subln[l].reshape(1, V_DIM)
        wo = w_out[l].astype(BF16)
        wpg = w_pg[l].astype(BF16)
        bpg = b_pg[l].reshape(1, d)
        wpp = w_pp[l].astype(BF16)
        st = state_conv[l]

        conv, q, k, kb, v, vt, sag, cst = _prompt_in(hp, g, wi, gsum, qw, kw, w_conv[l])
        conv_s, q_s, k_s, v_s, sag_s, u_s = _sample_in(
            hs, g, wi, gsum, qw, kw, w_conv[l], st[:, 0], st[:, 1])
        attn_lo, attn_hi, attn_s = _attn(q, kb, vt, sag, lam_vec[l], sub, page_table, q_s, k_s,
                                         v_s, sag_s, cache_k[l], cache_v[l], lam_init)
        hp = _out_proj(hp.reshape(b * s, d), conv.reshape(b * s, dc),
                       attn_lo.reshape(b * s // 2, da), attn_hi.reshape(b * s // 2, da),
                       p_prompt[l].reshape(b * s, -1), wo, wpg, bpg, wpp, s).reshape(b, s, d)
        hs = _out_proj(hs, conv_s, attn_s, attn_s, p_sample[l].reshape(db, -1), wo, wpg, bpg, wpp,
                       dt)
        outs[0].append(k.reshape(b, s, N_HEADS, 2 * HEAD_DIM))
        outs[1].append(v.reshape(b, s, N_HEADS, V_DIM))
        outs[2].append(cst)
        outs[3].append(k_s.reshape(db, 1, N_HEADS, 2 * HEAD_DIM))
        outs[4].append(v_s.reshape(db, 1, N_HEADS, V_DIM))
        outs[5].append(jnp.stack([st[:, 1], u_s], axis=1))

    return (hp, hs.reshape(db, 1, d)) + tuple(jnp.stack(o) for o in outs)
```

```python
import functools
import math

import jax
import jax.numpy as jnp
from jax import lax
from jax.experimental import pallas as pl
from jax.experimental.pallas import tpu as pltpu

F32 = jnp.float32
BF16 = jnp.bfloat16

N_HEADS = 4
HEAD_DIM = 64
V_DIM = 2 * HEAD_DIM
N_MAPS = 2 * N_HEADS
CONV_W = 3
EPS = 1e-6
LANES = 128
SUBLANES = 8
MXU_COLS = 256
VMEM_LIMIT = 56 * 1024 * 1024
Q_SCALE = HEAD_DIM ** -0.5 * math.log2(math.e)

TILE = 512
PAGE_PREFETCH = 4
PAGE_SLOTS = PAGE_PREFETCH + 1


def _lambda_init(layer):
    return 0.8 - 0.6 * math.exp(-0.3 * layer)


def _silu(x):
    return x * (1.0 / (1.0 + jnp.exp(-x)))


def _sigmoid(x):
    return 1.0 / (1.0 + jnp.exp(-x))


def _lam(lam_ref, lam_init):
    lv = lam_ref[...]
    a = jnp.sum(lv[0:1] * lv[1:2], axis=1, keepdims=True)
    b = jnp.sum(lv[2:3] * lv[3:4], axis=1, keepdims=True)
    return jnp.exp(a) - jnp.exp(b) + lam_init


def _normed_input(x, g):
    ms = jnp.mean(x * x, axis=-1, keepdims=True)
    return (x * lax.rsqrt(ms + EPS) * g).astype(BF16)


def _proj(xn, w_ref, i, width):
    return jnp.dot(xn, w_ref[:, i * width:(i + 1) * width], preferred_element_type=F32)


def _group_rmsnorm(t, gsum, w):
    ss = jnp.dot((t * t).astype(BF16), gsum, preferred_element_type=F32)
    return t * lax.rsqrt(ss * (1.0 / HEAD_DIM) + EPS) * w


def _head(x, h):
    return x[:, h * V_DIM:(h + 1) * V_DIM]


def _prompt_in_kernel(x_ref, g_ref, w_ref, gsum_ref, qw_ref, kw_ref, wc_ref,
                      conv_ref, q_ref, k_ref, kb_ref, v_ref, vt_ref, sag_ref, cst_ref,
                      ubuf):
    j = pl.program_id(1)
    tm = x_ref.shape[1]
    dc = conv_ref.shape[2]

    @pl.when(j == 0)
    def _():
        ubuf[0:SUBLANES, :] = jnp.zeros((SUBLANES, dc), F32)

    xn = _normed_input(x_ref[0], g_ref[...])
    cb = _proj(xn, w_ref, 0, dc)
    u = _proj(xn, w_ref, 1, dc) * _proj(xn, w_ref, 2, dc)
    gate = _silu(_proj(xn, w_ref, 3, dc))
    ubuf[SUBLANES:SUBLANES + tm, :] = u
    wc = wc_ref[...]
    conv = (wc[0:1] * ubuf[SUBLANES - 2:SUBLANES - 2 + tm, :]
            + wc[1:2] * ubuf[SUBLANES - 1:SUBLANES - 1 + tm, :]
            + wc[2:3] * u)
    conv_ref[0] = (cb * conv * gate).astype(BF16)
    ubuf[0:SUBLANES, :] = ubuf[tm:tm + SUBLANES, :]
    cst_ref[0] = u[tm - (CONV_W - 1):tm, :]

    gsum = gsum_ref[...]
    q = _group_rmsnorm(_proj(xn, w_ref, 4, dc), gsum, qw_ref[...])
    q_ref[0] = (q * Q_SCALE).astype(BF16)
    k = _group_rmsnorm(_proj(xn, w_ref, 5, dc), gsum, kw_ref[...])
    kb_ref[0] = k.astype(BF16)
    v = _proj(xn, w_ref, 6, dc)
    for h in range(N_HEADS):
        k_ref[0, pl.ds(h, tm, stride=N_HEADS), :] = _head(k, h)
        v_ref[0, pl.ds(h, tm, stride=N_HEADS), :] = _head(v, h)
        vt_ref[0, h, 0] = _head(v, h).T.astype(BF16)
    sag_ref[0] = _silu(_proj(xn, w_ref, 7, dc)).astype(BF16)


def _prompt_in(x, g, w_in, gsum, qw, kw, wc):
    b, s, d = x.shape
    dc = wc.shape[1]
    tm = TILE
    const = lambda shape: pl.BlockSpec(shape, lambda i, j: (0,) * len(shape))
    tile = lambda width: pl.BlockSpec((1, tm, width), lambda i, j: (i, j, 0))
    native = pl.BlockSpec((1, tm * N_HEADS, V_DIM), lambda i, j: (i, j, 0))
    out_shapes = (
        jax.ShapeDtypeStruct((b, s, dc), BF16),
        jax.ShapeDtypeStruct((b, s, dc), BF16),
        jax.ShapeDtypeStruct((b, s * N_HEADS, V_DIM), F32),
        jax.ShapeDtypeStruct((b, s, dc), BF16),
        jax.ShapeDtypeStruct((b, s * N_HEADS, V_DIM), F32),
        jax.ShapeDtypeStruct((b, N_HEADS, s // tm, V_DIM, tm), BF16),
        jax.ShapeDtypeStruct((b, s, dc), BF16),
        jax.ShapeDtypeStruct((b, CONV_W - 1, dc), F32),
    )
    out_specs = [tile(dc), tile(dc), native, tile(dc), native,
                 pl.BlockSpec((1, N_HEADS, 1, V_DIM, tm), lambda i, j: (i, 0, j, 0, 0)),
                 tile(dc),
                 pl.BlockSpec((1, CONV_W - 1, dc), lambda i, j: (i, 0, 0))]
    return pl.pallas_call(
        _prompt_in_kernel,
        grid=(b, s // tm),
        in_specs=[tile(d), const((1, d)), const(w_in.shape), const(gsum.shape),
                  const((1, dc)), const((1, dc)), const(wc.shape)],
        out_specs=out_specs,
        out_shape=out_shapes,
        scratch_shapes=[pltpu.VMEM((tm + 2 * SUBLANES, dc), F32)],
        compiler_params=pltpu.CompilerParams(
            dimension_semantics=("arbitrary", "arbitrary"), vmem_limit_bytes=VMEM_LIMIT),
        name="prompt_in",
    )(x, g, w_in, gsum, qw, kw, wc)


def _sample_in_kernel(x_ref, g_ref, w_ref, gsum_ref, qw_ref, kw_ref, wc_ref, st0_ref, st1_ref,
                      conv_ref, q_ref, k_ref, v_ref, sag_ref, u_ref):
    dc = conv_ref.shape[1]
    xn = _normed_input(x_ref[...], g_ref[...])
    cb = _proj(xn, w_ref, 0, dc)
    u = _proj(xn, w_ref, 1, dc) * _proj(xn, w_ref, 2, dc)
    gate = _silu(_proj(xn, w_ref, 3, dc))
    wc = wc_ref[...]
    conv = wc[0:1] * st0_ref[...] + wc[1:2] * st1_ref[...] + wc[2:3] * u
    conv_ref[...] = (cb * conv * gate).astype(BF16)
    u_ref[...] = u
    gsum = gsum_ref[...]
    q = _group_rmsnorm(_proj(xn, w_ref, 4, dc), gsum, qw_ref[...])
    q_ref[...] = (q * Q_SCALE).astype(BF16)
    k_ref[...] = _group_rmsnorm(_proj(xn, w_ref, 5, dc), gsum, kw_ref[...])
    v_ref[...] = _proj(xn, w_ref, 6, dc)
    sag_ref[...] = _silu(_proj(xn, w_ref, 7, dc)).astype(BF16)


def _sample_in(x, g, w_in, gsum, qw, kw, wc, st0, st1):
    n, d = x.shape
    dc = wc.shape[1]
    full = lambda a: pl.BlockSpec(a.shape, lambda i: (0,) * a.ndim)
    args = (x, g, w_in, gsum, qw, kw, wc, st0, st1)
    sds = lambda dt: jax.ShapeDtypeStruct((n, dc), dt)
    out_shape = (sds(BF16), sds(BF16), sds(F32), sds(F32), sds(BF16), sds(F32))
    return pl.pallas_call(
        _sample_in_kernel,
        grid=(1,),
        in_specs=[full(a) for a in args],
        out_specs=[pl.BlockSpec((n, dc), lambda i: (0, 0))] * len(out_shape),
        out_shape=out_shape,
        compiler_params=pltpu.CompilerParams(
            dimension_semantics=("arbitrary",), vmem_limit_bytes=VMEM_LIMIT),
        name="sample_in",
    )(*args)


def _per_map_rows(x):
    return jnp.concatenate([_head(x, h) for h in range(N_HEADS) for _ in range(2)], axis=0)


def _attn_kernel(pt_ref,
                 qa_ref, qb_ref, k_ref, vt_ref, saga_ref, sagb_ref, lam_ref, subcol_ref,
                 subrow_ref, dq_ref, dkn_ref, dvn_ref, dsag_ref, ck_hbm, cv_hbm,
                 oa_ref, ob_ref, od_ref,
                 m_ref, l_ref, acc_ref, s_ref, p_ref, a_ref,
                 kbuf, vbuf, ksem, vsem, dm_ref, dl_ref, dacc_ref, *, lam_init):
    i = pl.program_id(2)
    nq = vt_ref.shape[2]
    tq = qa_ref.shape[1]
    tk = tq
    seq = (pl.program_id(0) * N_HEADS + pl.program_id(1)) * (nq // 2) + i
    n_seq = pt_ref.shape[0]
    pages_per_chunk = kbuf.shape[1]
    rows_per_page = kbuf.shape[2]
    lam = _lam(lam_ref, lam_init)

    def page_copies(n):
        slot = lax.rem(n, PAGE_SLOTS)
        first_page = lax.rem(n, nq) * pages_per_chunk
        copies = []
        for k in range(pages_per_chunk):
            page = pt_ref[n // nq, first_page + k]
            copies.append(pltpu.make_async_copy(ck_hbm.at[page], kbuf.at[slot, k], ksem.at[slot]))
            copies.append(pltpu.make_async_copy(cv_hbm.at[page], vbuf.at[slot, k], vsem.at[slot]))
        return copies

    @pl.when(seq == 0)
    def _():
        for n in range(PAGE_PREFETCH):
            for cp in page_copies(n):
                cp.start()

    row = lax.broadcasted_iota(jnp.int32, (N_MAPS, V_DIM), 0)
    lane = lax.broadcasted_iota(jnp.int32, (N_MAPS, V_DIM), 1)
    own = (lane // HEAD_DIM) == (row % 2)
    dqf = jnp.where(own, _per_map_rows(dq_ref[0].astype(F32)), 0.0)
    dqz = dqf.astype(BF16)
    s_new = jnp.sum(dqf * _per_map_rows(dkn_ref[0]), axis=1, keepdims=True)
    dm_ref[...] = jnp.broadcast_to(s_new, dm_ref.shape)
    dl_ref[...] = jnp.ones(dl_ref.shape, F32)
    dacc_ref[...] = _per_map_rows(dvn_ref[0])

    def sample_chunk(c):
        n = seq * nq + c

        @pl.when(n + PAGE_PREFETCH < n_seq * nq)
        def _():
            for cp in page_copies(n + PAGE_PREFETCH):
                cp.start()

        for cp in page_copies(n):
            cp.wait()
        slot = lax.rem(n, PAGE_SLOTS)
        rows = pages_per_chunk * rows_per_page
        kc = kbuf[slot].reshape(rows, V_DIM).astype(BF16)
        s = lax.dot_general(dqz, kc, (((1,), (1,)), ((), ())), preferred_element_type=F32)
        yield
        srow = lax.broadcasted_iota(jnp.int32, s.shape, 0)
        scol = lax.broadcasted_iota(jnp.int32, s.shape, 1)
        s = jnp.where((scol % N_HEADS) == (srow // 2), s, -jnp.inf)
        m_prev = dm_ref[:, 0:1]
        m_next = jnp.maximum(m_prev, jnp.max(s, axis=1, keepdims=True))
        alpha = jnp.exp2(m_prev - m_next)
        p = jnp.exp2(s - m_next)
        l_next = alpha * dl_ref[:, 0:1] + jnp.sum(p, axis=1, keepdims=True)
        dm_ref[...] = jnp.broadcast_to(m_next, dm_ref.shape)
        dl_ref[...] = jnp.broadcast_to(l_next, dl_ref.shape)
        yield
        vc = vbuf[slot].reshape(rows, V_DIM).astype(BF16)
        dacc_ref[...] = alpha * dacc_ref[...] + jnp.dot(p.astype(BF16), vc,
                                                        preferred_element_type=F32)
        yield

    def sample_finish():
        o_all = dacc_ref[...] / dl_ref[...]
        for h in range(N_HEADS):
            sl = slice(h * V_DIM, (h + 1) * V_DIM)
            o = o_all[2 * h:2 * h + 1] - lam * o_all[2 * h + 1:2 * h + 2]
            ms = jnp.mean(o * o, axis=-1, keepdims=True)
            o = o * lax.rsqrt(ms + EPS) * subrow_ref[...] * (1.0 - lam_init)
            od_ref[0, :, sl] = (o * dsag_ref[0, :, sl].astype(F32)).astype(BF16)

    chunks = [slice(c * MXU_COLS, (c + 1) * MXU_COLS) for c in range(2 * tq // MXU_COLS)]

    def attend(q_ref, sag_ref, o_ref, qi, first_chunk, last_chunk):
        q = q_ref[0].astype(F32)
        qlane = lax.broadcasted_iota(jnp.int32, q.shape, 1)
        qst = jnp.concatenate([jnp.where(qlane < HEAD_DIM, q, 0.0),
                               jnp.where(qlane >= HEAD_DIM, q, 0.0)], axis=0).T.astype(BF16)

        m_ref[...] = jnp.full(m_ref.shape, -jnp.inf, F32)
        l_ref[...] = jnp.zeros(l_ref.shape, F32)
        acc_ref[...] = jnp.zeros(acc_ref.shape, F32)

        def scores(j, cols):
            start = pl.multiple_of(j * tk, tk)
            return jnp.dot(k_ref[0, pl.ds(start, tk), :], qst[:, cols],
                           preferred_element_type=F32)

        def softmax_update(cols, masked):
            s = s_ref[:, cols]
            if masked:
                key = lax.broadcasted_iota(jnp.int32, s.shape, 0)
                qry = (lax.broadcasted_iota(jnp.int32, s.shape, 1) + cols.start) & (tq - 1)
                s = jnp.where(key <= qry, s, -jnp.inf)
            m_prev = m_ref[:, cols]
            m_next = jnp.maximum(m_prev, jnp.max(s, axis=0, keepdims=True))
            alpha = jnp.exp2(m_prev - m_next)
            p = jnp.exp2(s - m_next)
            l_ref[:, cols] = alpha * l_ref[:, cols] + jnp.sum(p, axis=0, keepdims=True)
            m_ref[:, cols] = m_next
            return alpha, p.astype(BF16)

        def accumulate(cols, alpha, pb, vt):
            acc_ref[:, cols] = alpha * acc_ref[:, cols] + jnp.dot(vt, pb,
                                                                preferred_element_type=F32)

        for cols in chunks:
            s_ref[:, cols] = scores(0, cols)
        p_ref[...] = jnp.zeros(p_ref.shape, BF16)
        a_ref[...] = jnp.ones(a_ref.shape, F32)

        def body(t, carry):
            sample = sample_chunk(first_chunk + t)
            vt_prev = vt_ref[0, 0, jnp.maximum(t - 1, 0)]
            for cols in chunks:
                next(sample, None)
                accumulate(cols, a_ref[:, cols], p_ref[:, cols], vt_prev)
                alpha, pb = softmax_update(cols, False)
                p_ref[:, cols] = pb
                a_ref[:, cols] = alpha
                s_ref[:, cols] = scores(t + 1, cols)
            return carry

        lax.fori_loop(0, qi, body, 0)
        sample = iter(()) if last_chunk is None else sample_chunk(last_chunk)
        vt_prev = vt_ref[0, 0, jnp.maximum(qi - 1, 0)]
        vt_last = vt_ref[0, 0, qi]
        for cols in chunks:
            next(sample, None)
            accumulate(cols, a_ref[:, cols], p_ref[:, cols], vt_prev)
            alpha, pb = softmax_update(cols, True)
            accumulate(cols, alpha, pb, vt_last)

        ot = acc_ref[...] * (1.0 / l_ref[...])
        ot = ot[:, :tq] - lam * ot[:, tq:]
        ms = jnp.mean(ot * ot, axis=0, keepdims=True)
        ot = ot * lax.rsqrt(ms + EPS) * (subcol_ref[...] * (1.0 - lam_init))
        o_ref[0] = (ot.T * sag_ref[0].astype(F32)).astype(BF16)

    attend(qa_ref, saga_ref, oa_ref, i, 0, None)
    attend(qb_ref, sagb_ref, ob_ref, nq - 1 - i, i, nq - 1)
    sample_finish()


def _attn(q, kb, vt, sag, lam_vec, subln, page_table, dq, dkn, dvn, dsag, cache_k, cache_v,
          lam_init):
    b, s, da = q.shape
    t = vt.shape[-1]
    nq = s // t
    half = nq // 2
    n_seq, n_pages = page_table.shape
    n_pool, page = cache_k.shape[:2]
    assert V_DIM == LANES and da == N_HEADS * V_DIM and nq % 2 == 0
    assert n_seq == b * N_HEADS * half and n_pages % nq == 0
    pages_per_chunk = n_pages // nq
    rows_per_page = page * N_HEADS
    ck = cache_k.reshape(n_pool, rows_per_page, V_DIM)
    cv = cache_v.reshape(n_pool, rows_per_page, V_DIM)

    qtile = lambda idx: pl.BlockSpec((1, t, V_DIM), idx)
    lo = lambda bi, h, i, pt: (bi, i, h)
    hi = lambda bi, h, i, pt: (bi, nq - 1 - i, h)
    hi_out = lambda bi, h, i, pt: (bi, half - 1 - i, h)
    const = lambda shape: pl.BlockSpec(shape, lambda bi, h, i, pt: (0,) * len(shape))
    seq_row = pl.BlockSpec((1, 1, da), lambda bi, h, i, pt: ((bi * N_HEADS + h) * half + i, 0, 0))
    r3 = lambda a: a.reshape(n_seq, 1, da)
    chunk_buf = pltpu.VMEM((PAGE_SLOTS, pages_per_chunk, rows_per_page, V_DIM), F32)

    grid_spec = pltpu.PrefetchScalarGridSpec(
        num_scalar_prefetch=1,
        grid=(b, N_HEADS, half),
        in_specs=[qtile(lo), qtile(hi),
                  pl.BlockSpec((1, s, V_DIM), lambda bi, h, i, pt: (bi, 0, h)),
                  pl.BlockSpec((1, 1, nq, V_DIM, t), lambda bi, h, i, pt: (bi, h, 0, 0, 0)),
                  qtile(lo), qtile(hi),
                  const(lam_vec.shape), const((V_DIM, 1)), const((1, V_DIM)),
                  seq_row, seq_row, seq_row, seq_row,
                  pl.BlockSpec(memory_space=pl.ANY), pl.BlockSpec(memory_space=pl.ANY)],
        out_specs=[qtile(lo), qtile(hi_out), seq_row],
        scratch_shapes=[pltpu.VMEM((1, 2 * t), F32), pltpu.VMEM((1, 2 * t), F32),
                        pltpu.VMEM((V_DIM, 2 * t), F32), pltpu.VMEM((t, 2 * t), F32),
                        pltpu.VMEM((t, 2 * t), BF16), pltpu.VMEM((1, 2 * t), F32),
                        chunk_buf, chunk_buf,
                        pltpu.SemaphoreType.DMA((PAGE_SLOTS,)),
                        pltpu.SemaphoreType.DMA((PAGE_SLOTS,)),
                        pltpu.VMEM((N_MAPS, LANES), F32), pltpu.VMEM((N_MAPS, LANES), F32),
                        pltpu.VMEM((N_MAPS, V_DIM), F32)],
    )
    o_lo, o_hi, o_s = pl.pallas_call(
        functools.partial(_attn_kernel, lam_init=lam_init),
        grid_spec=grid_spec,
        out_shape=(jax.ShapeDtypeStruct((b, s // 2, da), BF16),
                   jax.ShapeDtypeStruct((b, s // 2, da), BF16),
                   jax.ShapeDtypeStruct((n_seq, 1, da), BF16)),
        compiler_params=pltpu.CompilerParams(
            dimension_semantics=("arbitrary",) * 3, vmem_limit_bytes=VMEM_LIMIT),
        name="attn",
    )(page_table, q, q, kb, vt, sag, sag, lam_vec, subln.reshape(V_DIM, 1), subln,
      r3(dq), r3(dkn), r3(dvn), r3(dsag), ck, cv)
    return o_lo, o_hi, o_s.reshape(n_seq, da)


def _out_kernel(x_ref, conv_ref, lo_ref, hi_ref, p_ref, wo_ref, wpg_ref, bpg_ref, wpp_ref, y_ref,
                *, tiles_per_seq):
    dc = conv_ref.shape[1]
    second_half = lax.rem(pl.program_id(0), tiles_per_seq) >= tiles_per_seq // 2
    attn = jnp.where(second_half, hi_ref[...], lo_ref[...])
    h = (x_ref[...]
         + jnp.dot(conv_ref[...], wo_ref[0:dc, :], preferred_element_type=F32)
         + jnp.dot(attn, wo_ref[dc:, :], preferred_element_type=F32))
    gate = _sigmoid(jnp.dot(h.astype(BF16), wpg_ref[...], preferred_element_type=F32)
                    + bpg_ref[...])
    emb = jnp.dot(p_ref[...].astype(BF16), wpp_ref[...], preferred_element_type=F32)
    y_ref[...] = h + gate * emb


def _out_proj(x, conv, attn_lo, attn_hi, p, wo, wpg, bpg, wpp, seq_len):
    n, d = x.shape
    tm = min(TILE, n)
    tps = max(seq_len // tm, 1)
    half = max(tps // 2, 1)
    const = lambda shape: pl.BlockSpec(shape, lambda i: (0,) * len(shape))
    tile = lambda width: pl.BlockSpec((tm, width), lambda i: (i, 0))
    da = attn_lo.shape[1]
    lo = pl.BlockSpec((tm, da), lambda i: ((i // tps) * half + jnp.minimum(i % tps, half - 1), 0))
    hi = pl.BlockSpec((tm, da), lambda i: ((i // tps) * half + jnp.maximum(i % tps - half, 0), 0))
    return pl.pallas_call(
        functools.partial(_out_kernel, tiles_per_seq=tps),
        grid=(n // tm,),
        in_specs=[tile(d), tile(conv.shape[1]), lo, hi, tile(p.shape[1]),
                  const(wo.shape), const(wpg.shape), const(bpg.shape), const(wpp.shape)],
        out_specs=tile(d),
        out_shape=jax.ShapeDtypeStruct((n, d), F32),
        compiler_params=pltpu.CompilerParams(
            dimension_semantics=("arbitrary",), vmem_limit_bytes=VMEM_LIMIT),
        name="out_proj",
    )(x, conv, attn_lo, attn_hi, p, wo, wpg, bpg, wpp)


def kernel(x_prompt, x_sample, p_prompt, p_sample, cache_k, cache_v, state_conv, page_table,
           g_norm, w_in, w_conv, q_norm, k_norm, lam_vec, subln, w_out, w_pg, b_pg, w_pp):
    depth = w_in.shape[0]
    b, s, d = x_prompt.shape
    db, dt, _ = x_sample.shape
    assert dt == 1
    dc = w_conv.shape[2]
    da = N_HEADS * V_DIM
    grp = jnp.arange(da) // HEAD_DIM
    gsum = (grp[:, None] == grp[None, :]).astype(BF16)

    hp = x_prompt
    hs = x_sample.reshape(db, d)
    outs = [[] for _ in range(6)]
    for l in range(depth):
        lam_init = _lambda_init(l)
        g = g_norm[l].reshape(1, d)
        wi = w_in[l].astype(BF16)
        qw = jnp.tile(q_norm[l], da // HEAD_DIM).reshape(1, da)
        kw = jnp.tile(k_norm[l], da // HEAD_DIM).reshape(1, da)
        sub = subln[l].reshape(1, V_DIM)
        wo = w_out[l].astype(BF16)
        wpg = w_pg[l].astype(BF16)
        bpg = b_pg[l].reshape(1, d)
        wpp = w_pp[l].astype(BF16)
        st = state_conv[l]

        conv, q, k, kb, v, vt, sag, cst = _prompt_in(hp, g, wi, gsum, qw, kw, w_conv[l])
        conv_s, q_s, k_s, v_s, sag_s, u_s = _sample_in(
            hs, g, wi, gsum, qw, kw, w_conv[l], st[:, 0], st[:, 1])
        attn_lo, attn_hi, attn_s = _attn(q, kb, vt, sag, lam_vec[l], sub, page_table, q_s, k_s,
                                         v_s, sag_s, cache_k[l], cache_v[l], lam_init)
        hp = _out_proj(hp.reshape(b * s, d), conv.reshape(b * s, dc),
                       attn_lo.reshape(b * s // 2, da), attn_hi.reshape(b * s // 2, da),
                       p_prompt[l].reshape(b * s, -1), wo, wpg, bpg, wpp, s).reshape(b, s, d)
        hs = _out_proj(hs, conv_s, attn_s, attn_s, p_sample[l].reshape(db, -1), wo, wpg, bpg, wpp,
                       dt)
        outs[0].append(k.reshape(b, s, N_HEADS, 2 * HEAD_DIM))
        outs[1].append(v.reshape(b, s, N_HEADS, V_DIM))
        outs[2].append(cst)
        outs[3].append(k_s.reshape(db, 1, N_HEADS, 2 * HEAD_DIM))
        outs[4].append(v_s.reshape(db, 1, N_HEADS, V_DIM))
        outs[5].append(jnp.stack([st[:, 1], u_s], axis=1))

    return (hp, hs.reshape(db, 1, d)) + tuple(jnp.stack(o) for o in outs)
```
